```python
import math
import jax
import jax.numpy as jnp
from jax import lax
import numpy as np


D_MODEL = 1024
BATCH = 32
SEQ = 2048
DEPTH = 2

CTX_LEN = 256
GRID_W = 64
HEAD_DIM = 64
ROPE_BASE = 10000.0
EPS = 1e-6
NEG_INF = -1e30
ATTN_SCALE = HEAD_DIM ** -0.5
Q_BLOCK = 128
N_EVEN = (DEPTH + 1) // 2
N_ODD = DEPTH // 2

A_HEADS = 8
A_KV_HEADS = 2
A_GROUP = A_HEADS // A_KV_HEADS
WINDOW = 128
A_WIDTH = A_HEADS * HEAD_DIM
B_HEADS = 4
B_VDIM = 2 * HEAD_DIM
B_WIDTH = B_HEADS * B_VDIM
EVEN_SPLITS = (A_WIDTH, A_KV_HEADS * HEAD_DIM, A_KV_HEADS * HEAD_DIM, A_WIDTH,
               B_HEADS * 2 * HEAD_DIM, B_HEADS * 2 * HEAD_DIM, B_WIDTH, B_WIDTH)
EVEN_IN = sum(EVEN_SPLITS)

G_HEADS = 4
G_DK = (D_MODEL // 2) // G_HEADS
G_DV = D_MODEL // G_HEADS
G_RANK = 16
G_TAU = 16.0
G_CHUNK = 64
ODD_SPLITS = (G_HEADS * G_DK, G_HEADS * G_DK, G_HEADS * G_DV, G_HEADS * G_DV, G_RANK, G_RANK)
ODD_IN = sum(ODD_SPLITS)

kernel_name = 'hybrid_window_diff_gla_prefix_dit'


def rms_norm(x, g):
    xf = x.astype(jnp.float32)
    y = xf * lax.rsqrt(jnp.mean(xf * xf, axis=-1, keepdims=True) + EPS)
    return (y * g.astype(jnp.float32)).astype(x.dtype)


def split_cols(p, sizes):
    out, start = [], 0
    for s in sizes:
        out.append(p[..., start:start + s])
        start += s
    return out


def adaln_params(cond, w, b):
    m = jax.nn.silu(cond) @ w + b
    return jnp.split(m, 3, axis=-1)


def grid_positions(n_rows):
    rows = jnp.repeat(jnp.arange(n_rows, dtype=jnp.int32), GRID_W)
    cols = jnp.tile(jnp.arange(GRID_W, dtype=jnp.int32), n_rows)
    return rows, cols


def rope_1d(x, pos):
    m = x.shape[-1] // 2
    inv = ROPE_BASE ** (-jnp.arange(m, dtype=jnp.float32) / m)
    ang = pos.astype(jnp.float32)[:, None] * inv[None, :]
    cos = jnp.cos(ang)[:, None, :]
    sin = jnp.sin(ang)[:, None, :]
    x1 = x[..., :m].astype(jnp.float32)
    x2 = x[..., m:].astype(jnp.float32)
    return jnp.concatenate([x1 * cos - x2 * sin, x2 * cos + x1 * sin], axis=-1)


def rope_axial(x, rows, cols):
    h = x.shape[-1] // 2
    return jnp.concatenate([rope_1d(x[..., :h], rows), rope_1d(x[..., h:], cols)], axis=-1).astype(x.dtype)


def sink_attention(qg, k, v, sink, valid):
    s = jnp.einsum('bqhgd,bkhd->bhgqk', qg, k).astype(jnp.float32) * ATTN_SCALE
    if valid is not None:
        s = jnp.where(valid, s, NEG_INF)
    sink_col = jnp.broadcast_to(sink.astype(jnp.float32).reshape(1, A_KV_HEADS, A_GROUP, 1, 1),
                                s.shape[:-1] + (1,))
    p = jax.nn.softmax(jnp.concatenate([s, sink_col], axis=-1), axis=-1)[..., :-1]
    return jnp.einsum('bhgqk,bkhd->bqhgd', p, v.astype(jnp.float32))


def window_sink_attention(q, k, v, kc, vc, sink):
    Bn, T = q.shape[:2]
    nb = T // Q_BLOCK
    band = Q_BLOCK + 2 * WINDOW
    qg = q.reshape(Bn, T, A_KV_HEADS, A_GROUP, HEAD_DIM)
    pad = ((0, 0), (WINDOW, WINDOW), (0, 0), (0, 0))
    kp = jnp.pad(k, pad)
    vp = jnp.pad(v, pad)
    ctx_valid = jnp.ones((Q_BLOCK, kc.shape[1]), dtype=bool)

    def block(i):
        start = i * Q_BLOCK
        qb = lax.dynamic_slice_in_dim(qg, start, Q_BLOCK, axis=1)
        kb = lax.dynamic_slice_in_dim(kp, start, band, axis=1)
        vb = lax.dynamic_slice_in_dim(vp, start, band, axis=1)
        qpos = start + jnp.arange(Q_BLOCK)
        kpos = start - WINDOW + jnp.arange(band)
        valid = ((kpos[None, :] >= 0) & (kpos[None, :] < T)
                 & (jnp.abs(qpos[:, None] - kpos[None, :]) <= WINDOW))
        valid = jnp.concatenate([valid, ctx_valid], axis=1)
        return sink_attention(qb, jnp.concatenate([kb, kc], axis=1),
                              jnp.concatenate([vb, vc], axis=1), sink, valid)

    out = lax.map(block, jnp.arange(nb))
    return jnp.moveaxis(out, 0, 1).reshape(Bn, T, A_KV_HEADS, A_GROUP, HEAD_DIM)


def diff_attention(q1, q2, k1, k2, v, lam):
    s1 = jnp.einsum('bqhd,bkhd->bhqk', q1, k1).astype(jnp.float32) * ATTN_SCALE
    s2 = jnp.einsum('bqhd,bkhd->bhqk', q2, k2).astype(jnp.float32) * ATTN_SCALE
    w = jax.nn.softmax(s1, axis=-1) - lam * jax.nn.softmax(s2, axis=-1)
    return jnp.einsum('bhqk,bkhe->bqhe', w, v.astype(jnp.float32))


def blocked_diff_attention(q1, q2, k1, k2, v, lam):
    Bn, T = q1.shape[:2]
    nb = T // Q_BLOCK

    def block(i):
        start = i * Q_BLOCK
        return diff_attention(lax.dynamic_slice_in_dim(q1, start, Q_BLOCK, axis=1),
                              lax.dynamic_slice_in_dim(q2, start, Q_BLOCK, axis=1),
                              k1, k2, v, lam)

    out = lax.map(block, jnp.arange(nb))
    return jnp.moveaxis(out, 0, 1).reshape(Bn, T, B_HEADS, B_VDIM)


def even_mixer(hx, hc, rows, cols, w_in, a_qn, a_kn, a_sink, b_qn, b_kn,
               lq1, lk1, lq2, lk2, b_subln, lambda_init, need_ctx):
    dt = hx.dtype

    def project(h):
        Bn, T = h.shape[:2]
        qa, ka, va, ga, qb, kb, vb, gb = split_cols(h @ w_in, EVEN_SPLITS)
        qa = rms_norm(qa.reshape(Bn, T, A_HEADS, HEAD_DIM), a_qn)
        ka = rms_norm(ka.reshape(Bn, T, A_KV_HEADS, HEAD_DIM), a_kn)
        va = va.reshape(Bn, T, A_KV_HEADS, HEAD_DIM)
        qb = rms_norm(qb.reshape(Bn, T, 2 * B_HEADS, HEAD_DIM), b_qn)
        kb = rms_norm(kb.reshape(Bn, T, 2 * B_HEADS, HEAD_DIM), b_kn)
        vb = vb.reshape(Bn, T, B_HEADS, B_VDIM)
        return qa, ka, va, ga, qb, kb, vb, gb

    def split_pair(t):
        Bn, T = t.shape[:2]
        t = t.reshape(Bn, T, B_HEADS, 2, HEAD_DIM)
        return t[..., 0, :], t[..., 1, :]

    def merge(oa, ga, ob, gb):
        Bn, T = oa.shape[:2]
        ob = rms_norm(ob, b_subln) * (1.0 - lambda_init)
        oa = oa.reshape(Bn, T, A_WIDTH).astype(dt) * jax.nn.silu(ga)
        ob = ob.reshape(Bn, T, B_WIDTH).astype(dt) * jax.nn.silu(gb)
        return jnp.concatenate([oa, ob], axis=-1)

    qa_x, ka_x, va_x, ga_x, qb_x, kb_x, vb_x, gb_x = project(hx)
    qa_c, ka_c, va_c, ga_c, qb_c, kb_c, vb_c, gb_c = project(hc)
    qa_x = rope_axial(qa_x, rows, cols)
    ka_x = rope_axial(ka_x, rows, cols)
    qb_x = rope_axial(qb_x, rows, cols)
    kb_x = rope_axial(kb_x, rows, cols)

    lam = (jnp.exp(jnp.sum(lq1.astype(jnp.float32) * lk1.astype(jnp.float32)))
           - jnp.exp(jnp.sum(lq2.astype(jnp.float32) * lk2.astype(jnp.float32))) + lambda_init)

    q1x, q2x = split_pair(qb_x)
    k1x, k2x = split_pair(kb_x)
    k1c, k2c = split_pair(kb_c)

    oa_x = window_sink_attention(qa_x, ka_x, va_x, ka_c, va_c, a_sink)
    ob_x = blocked_diff_attention(q1x, q2x,
                                  jnp.concatenate([k1x, k1c], axis=1),
                                  jnp.concatenate([k2x, k2c], axis=1),
                                  jnp.concatenate([vb_x, vb_c], axis=1), lam)
    ox = merge(oa_x, ga_x, ob_x, gb_x)
    if not need_ctx:
        return ox, None
    Bn, L = hc.shape[:2]
    oa_c = sink_attention(qa_c.reshape(Bn, L, A_KV_HEADS, A_GROUP, HEAD_DIM), ka_c, va_c, a_sink, None)
    q1c, q2c = split_pair(qb_c)
    ob_c = diff_attention(q1c, q2c, k1c, k2c, vb_c, lam)
    oc = merge(oa_c, ga_c, ob_c, gb_c)
    return ox, oc


def gla_chunked(q, k, v, log_a, s0, with_output):
    Bn, H, T, dk = q.shape
    dv = v.shape[-1]
    n = T // G_CHUNK
    f32 = jnp.float32

    def ch(t):
        return t.astype(f32).reshape(Bn, H, n, G_CHUNK, t.shape[-1])

    q, k, v, la = ch(q), ch(k), ch(v), ch(log_a)
    b = jnp.cumsum(la, axis=3)
    b_last = b[:, :, :, -1:, :]
    d_state = jnp.einsum('bhncd,bhnce->bhnde', k * jnp.exp(b_last - b), v)
    decay = jnp.exp(b_last[:, :, :, 0, :])

    def step(S, inp):
        dec, ds = inp
        return dec[..., None] * S + ds, S

    s_fin, s_prev = lax.scan(step, s0.astype(f32),
                             (jnp.moveaxis(decay, 2, 0), jnp.moveaxis(d_state, 2, 0)))
    if not with_output:
        return None, s_fin
    s_prev = jnp.moveaxis(s_prev, 0, 2)
    qd = q * jnp.exp(b)
    kd = k * jnp.exp(-b)
    tri = jnp.tril(jnp.ones((G_CHUNK, G_CHUNK), dtype=bool))
    att = jnp.where(tri, jnp.einsum('bhncd,bhnsd->bhncs', qd, kd), 0.0)
    o = (jnp.einsum('bhncs,bhnse->bhnce', att, v)
         + jnp.einsum('bhncd,bhnde->bhnce', qd, s_prev))
    return o.reshape(Bn, H, T, dv), s_fin


def gla_mixer(hx, hc, w_in, wa_f, ba_f, wa_b, ba_b, out_norm, need_ctx):
    dt = hx.dtype

    def project(h):
        Bn, T = h.shape[:2]
        q, k, v, g, rf, rb = split_cols(h @ w_in, ODD_SPLITS)

        def heads(t, d):
            return t.reshape(Bn, T, G_HEADS, d).transpose(0, 2, 1, 3)

        la_f = jax.nn.log_sigmoid((rf @ wa_f + ba_f).astype(jnp.float32)) / G_TAU
        la_b = jax.nn.log_sigmoid((rb @ wa_b + ba_b).astype(jnp.float32)) / G_TAU
        return (heads(q, G_DK) * (G_DK ** -0.5), heads(k, G_DK), heads(v, G_DV), g,
                heads(la_f, G_DK), heads(la_b, G_DK))

    def flip(t):
        return jnp.flip(t, axis=2)

    def finish(o, g):
        Bn, H, T, dv = o.shape
        o = rms_norm(o.transpose(0, 2, 1, 3), out_norm).reshape(Bn, T, H * dv)
        return o.astype(dt) * jax.nn.silu(g)

    qx, kx, vx, gx, lfx, lbx = project(hx)
    qc, kc, vc, gc, lfc, lbc = project(hc)
    Bn = hx.shape[0]
    s0 = jnp.zeros((Bn, G_HEADS, G_DK, G_DV), jnp.float32)
    oc_f, sc_f = gla_chunked(qc, kc, vc, lfc, s0, need_ctx)
    oc_b, sc_b = gla_chunked(flip(qc), flip(kc), flip(vc), flip(lbc), s0, need_ctx)
    ox_f, _ = gla_chunked(qx, kx, vx, lfx, sc_f, True)
    ox_b, _ = gla_chunked(flip(qx), flip(kx), flip(vx), flip(lbx), sc_b, True)
    ox = finish(ox_f + flip(ox_b), gx)
    if not need_ctx:
        return ox, None
    oc = finish(oc_f + flip(oc_b), gc)
    return ox, oc


def setup_inputs(seed: int = 0) -> dict:
    key = jax.random.key(seed)
    ks = jax.random.split(key, 32)
    f32 = jnp.float32

    def nrm(k, shape, scale):
        return jax.random.normal(k, shape, f32) * scale

    def gain(k, shape):
        return 1.0 + 0.1 * jax.random.normal(k, shape, f32)

    D = D_MODEL
    return {
        'x': nrm(ks[0], (BATCH, SEQ, D), 1.0),
        'c': nrm(ks[1], (BATCH, D), 1.0),
        'ctx': nrm(ks[2], (BATCH, CTX_LEN, D), 1.0),
        'c_ctx': nrm(ks[3], (D,), 1.0),
        'adaln_w': nrm(ks[4], (DEPTH, D, 3 * D), 0.5 * D ** -0.5),
        'adaln_b': nrm(ks[5], (DEPTH, 3 * D), 0.02),
        'norm_g': gain(ks[6], (DEPTH, D)),
        'w_out': nrm(ks[7], (DEPTH, D, D), D ** -0.5),
        'ab_w_in': nrm(ks[8], (N_EVEN, D, EVEN_IN), D ** -0.5),
        'a_q_norm': gain(ks[9], (N_EVEN, HEAD_DIM)),
        'a_k_norm': gain(ks[10], (N_EVEN, HEAD_DIM)),
        'a_sink': nrm(ks[11], (N_EVEN, A_HEADS), 0.5),
        'b_q_norm': gain(ks[12], (N_EVEN, HEAD_DIM)),
        'b_k_norm': gain(ks[13], (N_EVEN, HEAD_DIM)),
        'b_lambda_q1': nrm(ks[14], (N_EVEN, HEAD_DIM), 0.1),
        'b_lambda_k1': nrm(ks[15], (N_EVEN, HEAD_DIM), 0.1),
        'b_lambda_q2': nrm(ks[16], (N_EVEN, HEAD_DIM), 0.1),
        'b_lambda_k2': nrm(ks[17], (N_EVEN, HEAD_DIM), 0.1),
        'b_subln': gain(ks[18], (N_EVEN, B_VDIM)),
        'gla_w_in': nrm(ks[19], (N_ODD, D, ODD_IN), D ** -0.5),
        'gla_wa_f': nrm(ks[20], (N_ODD, G_RANK, G_HEADS * G_DK), G_RANK ** -0.5),
        'gla_ba_f': nrm(ks[21], (N_ODD, G_HEADS * G_DK), 0.1),
        'gla_wa_b': nrm(ks[22], (N_ODD, G_RANK, G_HEADS * G_DK), G_RANK ** -0.5),
        'gla_ba_b': nrm(ks[23], (N_ODD, G_HEADS * G_DK), 0.1),
        'gla_out_norm': gain(ks[24], (N_ODD, G_DV)),
    }


def reference(x, c, ctx, c_ctx, adaln_w, adaln_b, norm_g, w_out, ab_w_in, a_q_norm, a_k_norm,
              a_sink, b_q_norm, b_k_norm, b_lambda_q1, b_lambda_k1, b_lambda_q2, b_lambda_k2,
              b_subln, gla_w_in, gla_wa_f, gla_ba_f, gla_wa_b, gla_ba_b, gla_out_norm):
    ROWS = x.shape[1] // GRID_W
    rows, cols = grid_positions(ROWS)
    for layer in range(DEPTH):
        need_ctx = layer < DEPTH - 1
        shx, scx, gtx = adaln_params(c, adaln_w[layer], adaln_b[layer])
        shc, scc, gtc = adaln_params(c_ctx, adaln_w[layer], adaln_b[layer])
        hx = rms_norm(x, norm_g[layer]) * (1.0 + scx[:, None, :]) + shx[:, None, :]
        hc = rms_norm(ctx, norm_g[layer]) * (1.0 + scc) + shc
        j = layer // 2
        if layer % 2 == 0:
            lambda_init = 0.8 - 0.6 * math.exp(-0.3 * layer)
            ox, oc = even_mixer(hx, hc, rows, cols, ab_w_in[j], a_q_norm[j], a_k_norm[j], a_sink[j],
                                b_q_norm[j], b_k_norm[j], b_lambda_q1[j], b_lambda_k1[j],
                                b_lambda_q2[j], b_lambda_k2[j], b_subln[j], lambda_init, need_ctx)
        else:
            ox, oc = gla_mixer(hx, hc, gla_w_in[j], gla_wa_f[j], gla_ba_f[j], gla_wa_b[j], gla_ba_b[j],
                               gla_out_norm[j], need_ctx)
        x = x + gtx[:, None, :] * (ox @ w_out[layer])
        if need_ctx:
            ctx = ctx + gtc * (oc @ w_out[layer])
    return x
```

```python
import functools
import math

import jax
import jax.numpy as jnp
from jax import lax
from jax.experimental import pallas as pl
from jax.experimental.pallas import tpu as pltpu

D_MODEL = 1024
GRID_W = 64
HEAD_DIM = 64
ROPE_BASE = 10000.0
EPS = 1e-6
NEG_INF = -1e30
ATTN_SCALE = HEAD_DIM ** -0.5
WINDOW = 128
Q_BLOCK = 128

A_HEADS = 8
A_KV_HEADS = 2
A_GROUP = A_HEADS // A_KV_HEADS
A_WIDTH = A_HEADS * HEAD_DIM
A_KV_WIDTH = A_KV_HEADS * HEAD_DIM
B_HEADS = 4
B_VDIM = 2 * HEAD_DIM
B_WIDTH = B_HEADS * B_VDIM
EVEN_IN = 2 * A_WIDTH + 2 * A_KV_WIDTH + 4 * B_WIDTH

G_HEADS = 4
G_DK = 128
G_DV = 256
G_RANK = 16
G_TAU = 16.0
G_CHUNK = 64
G_QK = G_HEADS * G_DK
G_V = G_HEADS * G_DV
ODD_MAIN = 2 * G_QK + 2 * G_V
ODD_PAD = ODD_MAIN + 128

VMEM_LIMIT_BYTES = 56 * 1024 * 1024

BF16 = jnp.bfloat16
F32 = jnp.float32


def _params(*semantics):
    return pltpu.CompilerParams(dimension_semantics=semantics,
                                vmem_limit_bytes=VMEM_LIMIT_BYTES)


def _silu(v):
    return v * (1.0 / (1.0 + jnp.exp(-v)))


def _dot(a, b):
    return jnp.dot(a, b, preferred_element_type=F32)


def _dot_nt(a, b):
    return lax.dot_general(a, b, (((1,), (1,)), ((), ())), preferred_element_type=F32)


def _dot_tn(a, b):
    return lax.dot_general(a, b, (((0,), (0,)), ((), ())), preferred_element_type=F32)


def _split_bf16(v):
    hi = v.astype(BF16)
    lo = (v - hi.astype(F32)).astype(BF16)
    return hi, lo


def _adaln_kernel(cond_ref, w_ref, b_ref, o_ref):
    a = _silu(cond_ref[...])
    a_hi, a_lo = _split_bf16(a)
    w_hi, w_lo = _split_bf16(w_ref[...])
    o_ref[...] = (_dot(a_hi, w_hi) + _dot(a_hi, w_lo) + _dot(a_lo, w_hi)) + b_ref[...]


def _adaln(cond, w, b):
    n_layers = w.shape[0]
    rows = cond.shape[0]
    tn = 512
    return pl.pallas_call(
        _adaln_kernel,
        grid=(n_layers, 3 * D_MODEL // tn),
        in_specs=[
            pl.BlockSpec((rows, D_MODEL), lambda l, n: (0, 0)),
            pl.BlockSpec((None, D_MODEL, tn), lambda l, n: (l, 0, n)),
            pl.BlockSpec((None, 1, tn), lambda l, n: (l, 0, n)),
        ],
        out_specs=pl.BlockSpec((None, rows, tn), lambda l, n: (l, 0, n)),
        out_shape=jax.ShapeDtypeStruct((n_layers, rows, 3 * D_MODEL), F32),
        compiler_params=_params("arbitrary", "arbitrary"),
        name="adaln",
    )(cond, w, b)


def _modulated(x_ref, g_ref, sc_ref, sh_ref):
    xf = x_ref[...]
    r = lax.rsqrt(jnp.mean(xf * xf, axis=-1, keepdims=True) + EPS)
    return ((xf * r) * g_ref[...] * (1.0 + sc_ref[...]) + sh_ref[...]).astype(BF16)


def _mod_spec(per_batch):
    if per_batch:
        return pl.BlockSpec((None, 1, D_MODEL), lambda j, b: (b, 0, 0))
    return pl.BlockSpec((None, 1, D_MODEL), lambda j, b: (0, 0, 0))


def _rope_head(blk, c, s):
    r = lax.rsqrt(jnp.mean(blk * blk, axis=0, keepdims=True) + EPS)
    partner = jnp.concatenate([blk[16:32], blk[0:16], blk[48:64], blk[32:48]], axis=0)
    return (blk * c + partner * s) * r


def _even_proj_kernel(x_ref, g_ref, sc_ref, sh_ref, wt_ref, tab_ref,
                      qa_ref, ka_ref, va_ref, ga_ref, qb_ref, kb_ref, vb_ref, gb_ref):
    h = _modulated(x_ref, g_ref, sc_ref, sh_ref)

    def rows(start, size):
        return _dot_nt(wt_ref[start:start + size, :], h)

    def normed(acc, table):
        c = tab_ref[2 * table]
        s = tab_ref[2 * table + 1]
        n = acc.shape[0] // HEAD_DIM
        return jnp.concatenate(
            [_rope_head(acc[HEAD_DIM * i:HEAD_DIM * (i + 1)], c, s) for i in range(n)], axis=0)

    off = 0
    for i in range(A_WIDTH // 256):
        qa_ref[256 * i:256 * (i + 1), :] = normed(rows(off, 256), 0).astype(BF16)
        off += 256
    ka_ref[...] = normed(rows(off, A_KV_WIDTH), 1).T.astype(BF16)
    off += A_KV_WIDTH
    va_ref[...] = rows(off, A_KV_WIDTH).astype(BF16)
    off += A_KV_WIDTH
    for i in range(A_WIDTH // 256):
        ga_ref[256 * i:256 * (i + 1), :] = rows(off, 256).astype(BF16)
        off += 256
    for i in range(B_WIDTH // 256):
        qb_ref[256 * i:256 * (i + 1), :] = normed(rows(off, 256), 2).astype(BF16)
        off += 256
    for i in range(B_WIDTH // 256):
        kb_ref[:, 256 * i:256 * (i + 1)] = normed(rows(off, 256), 3).T.astype(BF16)
        off += 256
    for i in range(B_WIDTH // 256):
        vb_ref[256 * i:256 * (i + 1), :] = rows(off, 256).astype(BF16)
        off += 256
    for i in range(B_WIDTH // 256):
        gb_ref[256 * i:256 * (i + 1), :] = rows(off, 256).astype(BF16)
        off += 256


def _even_proj(x, norm_g, sc, sh, wt, tables, per_batch, tm):
    bn, t, _ = x.shape
    nt = t // tm

    def fm(width):
        return (pl.BlockSpec((None, width, tm), lambda j, b: (b, 0, j)),
                jax.ShapeDtypeStruct((bn, width, t), BF16))

    def tk(width):
        return (pl.BlockSpec((None, tm, width), lambda j, b: (b, j, 0)),
                jax.ShapeDtypeStruct((bn, t, width), BF16))

    outs = [fm(A_WIDTH), tk(A_KV_WIDTH), fm(A_KV_WIDTH), fm(A_WIDTH),
            fm(B_WIDTH), tk(B_WIDTH), fm(B_WIDTH), fm(B_WIDTH)]
    return pl.pallas_call(
        _even_proj_kernel,
        grid=(nt, bn),
        in_specs=[
            pl.BlockSpec((None, tm, D_MODEL), lambda j, b: (b, j, 0)),
            pl.BlockSpec((1, D_MODEL), lambda j, b: (0, 0)),
            _mod_spec(per_batch), _mod_spec(per_batch),
            pl.BlockSpec((EVEN_IN, D_MODEL), lambda j, b: (0, 0)),
            pl.BlockSpec((8, HEAD_DIM, tm), lambda j, b: (0, 0, j)),
        ],
        out_specs=[o[0] for o in outs],
        out_shape=[o[1] for o in outs],
        compiler_params=_params("arbitrary", "arbitrary"),
        name="even_proj",
    )(x, norm_g, sc, sh, wt, tables)


def _rope_tables(gains, t, rotary, q_scale):
    m = HEAD_DIM // 4
    inv = ROPE_BASE ** (-jnp.arange(m, dtype=F32) / m)
    pos = jnp.arange(t, dtype=jnp.int32)
    rows = (pos // GRID_W).astype(F32)
    cols = (pos % GRID_W).astype(F32)
    if rotary:
        ang_r = inv[:, None] * rows[None, :]
        ang_c = inv[:, None] * cols[None, :]
        cos = jnp.concatenate([jnp.cos(ang_r), jnp.cos(ang_r), jnp.cos(ang_c), jnp.cos(ang_c)], axis=0)
        sin = jnp.concatenate([-jnp.sin(ang_r), jnp.sin(ang_r), -jnp.sin(ang_c), jnp.sin(ang_c)], axis=0)
    else:
        cos = jnp.ones((HEAD_DIM, t), F32)
        sin = jnp.zeros((HEAD_DIM, t), F32)
    out = []
    for g, scale in zip(gains, q_scale):
        g = g.astype(F32)
        gp = jnp.concatenate([g[16:32], g[0:16], g[48:64], g[32:48]])
        out.append(g[:, None] * cos * scale)
        out.append(gp[:, None] * sin * scale)
    return jnp.stack(out)


def _win_attn_kernel(*refs, tq, windowed):
    if windowed:
        (q_ref, kp_ref, kc_ref, kn_ref, kx_ref, vp_ref, vc_ref, vn_ref, vx_ref,
         g_ref, sink_ref, o_ref, qpad_ref) = refs
        segs = [(kp_ref, vp_ref, "prev"), (kc_ref, vc_ref, None), (kn_ref, vn_ref, "next"),
                (kx_ref, vx_ref, None)]
    else:
        q_ref, kx_ref, vx_ref, g_ref, sink_ref, o_ref, qpad_ref = refs
        segs = [(kx_ref, vx_ref, None)]
    width = A_HEADS * tq
    qpad_ref[...] = jnp.zeros_like(qpad_ref)
    for hd in range(A_HEADS):
        grp = hd // A_GROUP
        qpad_ref[HEAD_DIM * grp:HEAD_DIM * (grp + 1), tq * hd:tq * (hd + 1)] = (
            q_ref[HEAD_DIM * hd:HEAD_DIM * (hd + 1), :])
    qpad = qpad_ref[...]

    scores = []
    for k_ref, _, kind in segs:
        s = _dot(k_ref[...], qpad)
        if kind is not None:
            i = pl.program_id(1)
            r = lax.broadcasted_iota(jnp.int32, s.shape, 0)
            cq = lax.broadcasted_iota(jnp.int32, s.shape, 1) % tq
            if kind == "prev":
                ok = r >= cq + jnp.where(i > 0, 0, tq)
            else:
                ok = r <= cq - jnp.where(i < pl.num_programs(1) - 1, 0, tq)
            s = jnp.where(ok, s, NEG_INF)
        scores.append(s)
    sink = sink_ref[...]
    m = sink
    for s in scores:
        m = jnp.maximum(m, jnp.max(s, axis=0, keepdims=True))
    l = jnp.exp(sink - m)
    acc = jnp.zeros((A_KV_WIDTH, width), F32)
    for s, (_, v_ref, _) in zip(scores, segs):
        p = jnp.exp(s - m)
        l = l + jnp.sum(p, axis=0, keepdims=True)
        acc = acc + _dot(v_ref[...], p.astype(BF16))
    inv = 1.0 / l
    for hd in range(A_HEADS):
        grp = hd // A_GROUP
        o = acc[HEAD_DIM * grp:HEAD_DIM * (grp + 1), tq * hd:tq * (hd + 1)]
        o = o * inv[:, tq * hd:tq * (hd + 1)]
        gate = g_ref[HEAD_DIM * hd:HEAD_DIM * (hd + 1), :].astype(F32)
        o_ref[HEAD_DIM * hd:HEAD_DIM * (hd + 1), :] = (o.astype(F32) * _silu(gate)).astype(BF16)


def _win_attn(qa, ka, va, kac, vac, ga, sink_row, windowed):
    bn, _, t = qa.shape
    ctx_len = kac.shape[1]
    if windowed:
        tq = Q_BLOCK
        nb = t // tq
        prev = lambda b, i: jnp.maximum(i - 1, 0)
        nxt = lambda b, i: jnp.minimum(i + 1, nb - 1)
        k_specs = [pl.BlockSpec((None, tq, A_KV_WIDTH), lambda b, i: (b, prev(b, i), 0)),
                   pl.BlockSpec((None, tq, A_KV_WIDTH), lambda b, i: (b, i, 0)),
                   pl.BlockSpec((None, tq, A_KV_WIDTH), lambda b, i: (b, nxt(b, i), 0)),
                   pl.BlockSpec((None, ctx_len, A_KV_WIDTH), lambda b, i: (b, 0, 0))]
        v_specs = [pl.BlockSpec((None, A_KV_WIDTH, tq), lambda b, i: (b, 0, prev(b, i))),
                   pl.BlockSpec((None, A_KV_WIDTH, tq), lambda b, i: (b, 0, i)),
                   pl.BlockSpec((None, A_KV_WIDTH, tq), lambda b, i: (b, 0, nxt(b, i))),
                   pl.BlockSpec((None, A_KV_WIDTH, ctx_len), lambda b, i: (b, 0, 0))]
        args = [qa, ka, ka, ka, kac, va, va, va, vac, ga, sink_row]
    else:
        tq = t
        nb = 1
        k_specs = [pl.BlockSpec((None, ctx_len, A_KV_WIDTH), lambda b, i: (b, 0, 0))]
        v_specs = [pl.BlockSpec((None, A_KV_WIDTH, ctx_len), lambda b, i: (b, 0, 0))]
        args = [qa, kac, vac, ga, sink_row]
    q_spec = pl.BlockSpec((None, A_WIDTH, tq), lambda b, i: (b, 0, i))
    return pl.pallas_call(
        functools.partial(_win_attn_kernel, tq=tq, windowed=windowed),
        grid=(bn, nb),
        in_specs=[q_spec] + k_specs + v_specs + [
            q_spec, pl.BlockSpec((1, A_HEADS * tq), lambda b, i: (0, 0))],
        out_specs=q_spec,
        out_shape=jax.ShapeDtypeStruct((bn, A_WIDTH, t), BF16),
        scratch_shapes=[pltpu.VMEM((A_KV_WIDTH, A_HEADS * tq), BF16)],
        compiler_params=_params("arbitrary", "arbitrary"),
        name="win_attn" if windowed else "ctx_sink_attn",
    )(*args)


def _diff_attn_kernel(*refs, tq, key_chunk, with_latent):
    if with_latent:
        (q_ref, k_ref, kx_ref, v_ref, vx_ref, g_ref, w_ref, lam_ref, o_ref, qpad_ref, s_ref) = refs
        segs = [(k_ref, v_ref), (kx_ref, vx_ref)]
    else:
        (q_ref, kx_ref, vx_ref, g_ref, w_ref, lam_ref, o_ref, qpad_ref, s_ref) = refs
        segs = [(kx_ref, vx_ref)]
    qpad_ref[...] = jnp.zeros_like(qpad_ref)
    qpad_ref[0:HEAD_DIM, 0:tq] = q_ref[0:HEAD_DIM, :]
    qpad_ref[HEAD_DIM:, tq:] = q_ref[HEAD_DIM:, :]
    qpad = qpad_ref[...]

    chunks = []
    base = 0
    for k_ref, v_ref in segs:
        n = k_ref.shape[0]
        for c0 in range(0, n, key_chunk):
            size = min(key_chunk, n - c0)
            chunks.append((k_ref, v_ref, c0, size, base + c0))
        base += n
    m = jnp.full((1, 2 * tq), NEG_INF, F32)
    for k_ref, _, c0, size, row in chunks:
        s = _dot(k_ref[c0:c0 + size, :], qpad)
        s_ref[row:row + size, :] = s
        m = jnp.maximum(m, jnp.max(s, axis=0, keepdims=True))
    l = jnp.zeros((1, 2 * tq), F32)
    acc = jnp.zeros((B_VDIM, 2 * tq), F32)
    for _, v_ref, c0, size, row in chunks:
        p = jnp.exp(s_ref[row:row + size, :] - m)
        l = l + jnp.sum(p, axis=0, keepdims=True)
        acc = acc + _dot(v_ref[:, c0:c0 + size], p.astype(BF16))
    acc = acc * (1.0 / l)
    o = acc[:, 0:tq] - lam_ref[...] * acc[:, tq:]
    r = lax.rsqrt(jnp.mean(o * o, axis=0, keepdims=True) + EPS)
    o = (o * r) * w_ref[...]
    o_ref[...] = (o * _silu(g_ref[...].astype(F32))).astype(BF16)


def _diff_attn(qb, kb, vb, kbc, vbc, gb, w_sub, lam_row, tq, with_latent):
    bn, _, t = qb.shape
    ctx_len = kbc.shape[1]
    nq = t // tq
    head_fm = lambda width: pl.BlockSpec((None, B_VDIM, width), lambda b, h, j: (b, h, 0))
    q_spec = pl.BlockSpec((None, B_VDIM, tq), lambda b, h, j: (b, h, j))
    kx_spec = pl.BlockSpec((None, ctx_len, B_VDIM), lambda b, h, j: (b, 0, h))
    if with_latent:
        t_keys = kb.shape[1]
        in_specs = [q_spec,
                    pl.BlockSpec((None, t_keys, B_VDIM), lambda b, h, j: (b, 0, h)), kx_spec,
                    head_fm(t_keys), head_fm(ctx_len)]
        args = [qb, kb, kbc, vb, vbc]
        n_keys = t_keys + ctx_len
    else:
        in_specs = [q_spec, kx_spec, head_fm(ctx_len)]
        args = [qb, kbc, vbc]
        n_keys = ctx_len
    in_specs += [q_spec,
                 pl.BlockSpec((B_VDIM, tq), lambda b, h, j: (0, 0)),
                 pl.BlockSpec((1, tq), lambda b, h, j: (0, 0))]
    args += [gb, w_sub, lam_row]
    return pl.pallas_call(
        functools.partial(_diff_attn_kernel, tq=tq, key_chunk=512, with_latent=with_latent),
        grid=(bn, B_HEADS, nq),
        in_specs=in_specs,
        out_specs=q_spec,
        out_shape=jax.ShapeDtypeStruct((bn, B_WIDTH, t), BF16),
        scratch_shapes=[pltpu.VMEM((B_VDIM, 2 * tq), BF16),
                        pltpu.VMEM((n_keys, 2 * tq), F32)],
        compiler_params=_params("arbitrary", "arbitrary", "arbitrary"),
        name="diff_attn" if with_latent else "ctx_diff_attn",
    )(*args)


def _even_out_kernel(oa_ref, ob_ref, w_ref, x_ref, gt_ref, o_ref):
    y = _dot_tn(oa_ref[...], w_ref[0:A_WIDTH, :]) + _dot_tn(ob_ref[...], w_ref[A_WIDTH:, :])
    o_ref[...] = x_ref[...] + gt_ref[...] * y


def _even_out(oa, ob, w, x, gate, per_batch, tm):
    bn, t, _ = x.shape
    fm = pl.BlockSpec((None, A_WIDTH, tm), lambda j, b: (b, 0, j))
    xs = pl.BlockSpec((None, tm, D_MODEL), lambda j, b: (b, j, 0))
    return pl.pallas_call(
        _even_out_kernel,
        grid=(t // tm, bn),
        in_specs=[fm, fm, pl.BlockSpec((D_MODEL, D_MODEL), lambda j, b: (0, 0)), xs,
                  _mod_spec(per_batch)],
        out_specs=xs,
        out_shape=jax.ShapeDtypeStruct(x.shape, F32),
        compiler_params=_params("arbitrary", "arbitrary"),
        name="even_out",
    )(oa, ob, w, x, gate)


def _odd_proj_kernel(x_ref, g_ref, sc_ref, sh_ref, w_ref, wab_ref, bab_ref,
                     q_ref, k_ref, v_ref, gate_ref, laf_ref, lab_ref):
    h = _modulated(x_ref, g_ref, sc_ref, sh_ref)

    def cols(start, size):
        return _dot(h, w_ref[:, start:start + size])

    q_ref[...] = (cols(0, G_QK) * (G_DK ** -0.5)).astype(BF16)
    k_ref[...] = cols(G_QK, G_QK).astype(BF16)
    for i in range(G_V // 512):
        v_ref[:, 512 * i:512 * (i + 1)] = cols(2 * G_QK + 512 * i, 512).astype(BF16)
    for i in range(G_V // 512):
        gate_ref[:, 512 * i:512 * (i + 1)] = cols(2 * G_QK + G_V + 512 * i, 512).astype(BF16)
    low = cols(ODD_MAIN, 128).astype(BF16)
    z = _dot(low, wab_ref[...]) + bab_ref[...]
    la = (jnp.minimum(z, 0.0) - jnp.log(1.0 + jnp.exp(-jnp.abs(z)))) / G_TAU
    laf_ref[...] = la[:, 0:G_QK]
    lab_ref[...] = la[:, G_QK:]


def _odd_proj(x, norm_g, sc, sh, w, wab, bab, per_batch, tm):
    bn, t, _ = x.shape

    def tk(width, dtype):
        return (pl.BlockSpec((None, tm, width), lambda j, b: (b, j, 0)),
                jax.ShapeDtypeStruct((bn, t, width), dtype))

    outs = [tk(G_QK, BF16), tk(G_QK, BF16), tk(G_V, BF16), tk(G_V, BF16), tk(G_QK, F32), tk(G_QK, F32)]
    return pl.pallas_call(
        _odd_proj_kernel,
        grid=(t // tm, bn),
        in_specs=[
            pl.BlockSpec((None, tm, D_MODEL), lambda j, b: (b, j, 0)),
            pl.BlockSpec((1, D_MODEL), lambda j, b: (0, 0)),
            _mod_spec(per_batch), _mod_spec(per_batch),
            pl.BlockSpec((D_MODEL, ODD_PAD), lambda j, b: (0, 0)),
            pl.BlockSpec((128, 2 * G_QK), lambda j, b: (0, 0)),
            pl.BlockSpec((1, 2 * G_QK), lambda j, b: (0, 0)),
        ],
        out_specs=[o[0] for o in outs],
        out_shape=[o[1] for o in outs],
        compiler_params=_params("arbitrary", "arbitrary"),
        name="odd_proj",
    )(x, norm_g, sc, sh, w, wab, bab)


def _gla_chunk(q, k, v, la, state_t, tri, causal_mask, with_out):
    la_hi, la_lo = _split_bf16(la)
    b = _dot(tri, la_hi) + _dot(tri, la_lo)
    b_last = jnp.sum(la, axis=0, keepdims=True)
    kf = k.astype(F32)
    k_state = (kf * jnp.exp(b_last - b)).astype(BF16)
    new_state = state_t * jnp.exp(b_last) + _dot_tn(v, k_state)
    if not with_out:
        return None, new_state
    qd = (q.astype(F32) * jnp.exp(b)).astype(BF16)
    kd = (kf * jnp.exp(-b)).astype(BF16)
    att = jnp.where(causal_mask, _dot_nt(qd, kd), 0.0).astype(BF16)
    o = _dot(att, v) + _dot_nt(qd, state_t.astype(BF16))
    return o, new_state


def _gla_kernel(q_ref, k_ref, v_ref, laf_ref, lab_ref, g_ref,
                qc_ref, kc_ref, vc_ref, lafc_ref, labc_ref, w_ref, o_ref, of_ref):
    n_lat = q_ref.shape[0] // G_CHUNK
    n_ctx = qc_ref.shape[0] // G_CHUNK
    r = lax.broadcasted_iota(jnp.int32, (G_CHUNK, G_CHUNK), 0)
    c = lax.broadcasted_iota(jnp.int32, (G_CHUNK, G_CHUNK), 1)
    fwd_mask = r >= c
    bwd_mask = r <= c
    tri_f = jnp.where(fwd_mask, 1.0, 0.0).astype(BF16)
    tri_b = jnp.where(bwd_mask, 1.0, 0.0).astype(BF16)
    zero_state = jnp.zeros((G_DV, G_DK), F32)

    def rows(i):
        return pl.ds(pl.multiple_of(i * G_CHUNK, G_CHUNK), G_CHUNK)

    state = zero_state
    for i in range(n_ctx):
        sl = slice(i * G_CHUNK, (i + 1) * G_CHUNK)
        _, state = _gla_chunk(qc_ref[sl, :], kc_ref[sl, :], vc_ref[sl, :], lafc_ref[sl, :],
                              state, tri_f, fwd_mask, False)

    def fwd_body(i, st):
        o, st = _gla_chunk(q_ref[rows(i), :], k_ref[rows(i), :], v_ref[rows(i), :],
                           laf_ref[rows(i), :], st, tri_f, fwd_mask, True)
        of_ref[rows(i), :] = o
        return st

    lax.fori_loop(0, n_lat, fwd_body, state)

    state = zero_state
    for i in reversed(range(n_ctx)):
        sl = slice(i * G_CHUNK, (i + 1) * G_CHUNK)
        _, state = _gla_chunk(qc_ref[sl, :], kc_ref[sl, :], vc_ref[sl, :], labc_ref[sl, :],
                              state, tri_b, bwd_mask, False)

    def bwd_body(n, st):
        i = n_lat - 1 - n
        o, st = _gla_chunk(q_ref[rows(i), :], k_ref[rows(i), :], v_ref[rows(i), :],
                           lab_ref[rows(i), :], st, tri_b, bwd_mask, True)
        o = o + of_ref[rows(i), :]
        rn = lax.rsqrt(jnp.mean(o * o, axis=-1, keepdims=True) + EPS)
        o = (o * rn) * w_ref[...]
        o_ref[rows(i), :] = (o * _silu(g_ref[rows(i), :].astype(F32))).astype(BF16)
        return st

    lax.fori_loop(0, n_lat, bwd_body, state)


def _gla(q, k, v, laf, lab, gate, qc, kc, vc, lafc, labc, w_norm):
    bn, t, _ = q.shape
    ctx_len = qc.shape[1]
    lat = lambda width: pl.BlockSpec((None, t, width), lambda b, h: (b, 0, h))
    ctx = lambda width: pl.BlockSpec((None, ctx_len, width), lambda b, h: (b, 0, h))
    return pl.pallas_call(
        _gla_kernel,
        grid=(bn, G_HEADS),
        in_specs=[lat(G_DK), lat(G_DK), lat(G_DV), lat(G_DK), lat(G_DK), lat(G_DV),
                  ctx(G_DK), ctx(G_DK), ctx(G_DV), ctx(G_DK), ctx(G_DK),
                  pl.BlockSpec((1, G_DV), lambda b, h: (0, 0))],
        out_specs=lat(G_DV),
        out_shape=jax.ShapeDtypeStruct((bn, t, G_V), BF16),
        scratch_shapes=[pltpu.VMEM((t, G_DV), F32)],
        compiler_params=_params("arbitrary", "arbitrary"),
        name="gla",
    )(q, k, v, laf, lab, gate, qc, kc, vc, lafc, labc, w_norm)


def _odd_out_kernel(o_ref, w_ref, x_ref, gt_ref, out_ref):
    out_ref[...] = x_ref[...] + gt_ref[...] * _dot(o_ref[...], w_ref[...])


def _odd_out(o, w, x, gate, tm):
    bn, t, _ = x.shape
    xs = pl.BlockSpec((None, tm, D_MODEL), lambda j, b: (b, j, 0))
    return pl.pallas_call(
        _odd_out_kernel,
        grid=(t // tm, bn),
        in_specs=[xs, pl.BlockSpec((D_MODEL, D_MODEL), lambda j, b: (0, 0)), xs, _mod_spec(True)],
        out_specs=xs,
        out_shape=jax.ShapeDtypeStruct(x.shape, F32),
        compiler_params=_params("arbitrary", "arbitrary"),
        name="odd_out",
    )(o, w, x, gate)


def _token_tile(t):
    return 512 if t % 512 == 0 else t


def kernel(x, c, ctx, c_ctx, adaln_w, adaln_b, norm_g, w_out, ab_w_in, a_q_norm, a_k_norm, a_sink,
           b_q_norm, b_k_norm, b_lambda_q1, b_lambda_k1, b_lambda_q2, b_lambda_k2, b_subln,
           gla_w_in, gla_wa_f, gla_ba_f, gla_wa_b, gla_ba_b, gla_out_norm):
    bn, t, _ = x.shape
    ctx_len = ctx.shape[1]
    depth = adaln_w.shape[0]
    assert depth == 2

    pad_rows = (-(bn + 1)) % 8
    cond = jnp.concatenate([c, c_ctx[None, :], jnp.zeros((pad_rows, D_MODEL), F32)], axis=0)
    mod = _adaln(cond, adaln_w, adaln_b[:, None, :])

    def mods(layer):
        m = mod[layer]
        shift, scale, gate = (m[:, i * D_MODEL:(i + 1) * D_MODEL] for i in range(3))
        per_x = tuple(v[:bn, None, :] for v in (shift, scale, gate))
        per_c = tuple(v[bn:bn + 1, None, :] for v in (shift, scale, gate))
        return per_x, per_c

    (shx, scx, gtx), (shc, scc, gtc) = mods(0)
    lambda_init = 0.8 - 0.6 * math.exp(-0.3 * 0)
    wt = ab_w_in[0].T.astype(BF16)
    gains = [a_q_norm[0], a_k_norm[0], b_q_norm[0], b_k_norm[0]]
    q_scale = [ATTN_SCALE, 1.0, ATTN_SCALE, 1.0]
    tab_x = _rope_tables(gains, t, True, q_scale)
    tab_c = _rope_tables(gains, ctx_len, False, q_scale)
    g0 = norm_g[0][None, :]
    qa, ka, va, ga, qb, kb, vb, gb = _even_proj(x, g0, scx, shx, wt, tab_x, True, _token_tile(t))
    qac, kac, vac, gac, qbc, kbc, vbc, gbc = _even_proj(ctx, g0, scc, shc, wt, tab_c, False,
                                                        _token_tile(ctx_len))

    sink = a_sink[0].astype(F32)
    oa = _win_attn(qa, ka, va, kac, vac, ga, jnp.repeat(sink, Q_BLOCK)[None, :], True)
    oac = _win_attn(qac, None, None, kac, vac, gac, jnp.repeat(sink, ctx_len)[None, :], False)

    lam = (jnp.exp(jnp.sum(b_lambda_q1[0].astype(F32) * b_lambda_k1[0].astype(F32)))
           - jnp.exp(jnp.sum(b_lambda_q2[0].astype(F32) * b_lambda_k2[0].astype(F32))) + lambda_init)
    tq = 256
    w_sub = jnp.broadcast_to((b_subln[0].astype(F32) * (1.0 - lambda_init))[:, None], (B_VDIM, tq))
    lam_row = jnp.broadcast_to(lam, (1, tq)).astype(F32)
    ob = _diff_attn(qb, kb, vb, kbc, vbc, gb, w_sub, lam_row, tq, True)
    obc = _diff_attn(qbc, None, None, kbc, vbc, gbc, w_sub, lam_row, tq, False)

    w0 = w_out[0].astype(BF16)
    x = _even_out(oa, ob, w0, x, gtx, True, _token_tile(t))
    ctx = _even_out(oac, obc, w0, ctx, gtc, False, _token_tile(ctx_len))

    (shx, scx, gtx), (shc, scc, _) = mods(1)
    w1 = jnp.pad(gla_w_in[0], ((0, 0), (0, ODD_PAD - gla_w_in.shape[2]))).astype(BF16)
    wab = jnp.zeros((128, 2 * G_QK), F32)
    wab = wab.at[0:G_RANK, 0:G_QK].set(gla_wa_f[0])
    wab = wab.at[G_RANK:2 * G_RANK, G_QK:].set(gla_wa_b[0]).astype(BF16)
    bab = jnp.concatenate([gla_ba_f[0], gla_ba_b[0]])[None, :].astype(F32)
    g1 = norm_g[1][None, :]
    q, k, v, gate, laf, lab = _odd_proj(x, g1, scx, shx, w1, wab, bab, True, _token_tile(t))
    qc, kc, vc, _, lafc, labc = _odd_proj(ctx, g1, scc, shc, w1, wab, bab, False, _token_tile(ctx_len))
    o = _gla(q, k, v, laf, lab, gate, qc, kc, vc, lafc, labc, gla_out_norm[0][None, :].astype(F32))
    return _odd_out(o, w_out[1].astype(BF16), x, gtx, _token_tile(t))
```

```python
import functools
import math

import jax
import jax.numpy as jnp
from jax import lax
from jax.experimental import pallas as pl
from jax.experimental.pallas import tpu as pltpu

D_MODEL = 1024
GRID_W = 64
HEAD_DIM = 64
ROPE_BASE = 10000.0
EPS = 1e-6
NEG_INF = -1e30
ATTN_SCALE = HEAD_DIM ** -0.5
LOG2E = math.log2(math.e)
WINDOW = 128
Q_BLOCK = 128

A_HEADS = 8
A_KV_HEADS = 2
A_GROUP = A_HEADS // A_KV_HEADS
A_WIDTH = A_HEADS * HEAD_DIM
A_KV_WIDTH = A_KV_HEADS * HEAD_DIM
B_HEADS = 4
B_VDIM = 2 * HEAD_DIM
B_WIDTH = B_HEADS * B_VDIM
EVEN_IN = 2 * A_WIDTH + 2 * A_KV_WIDTH + 4 * B_WIDTH

G_HEADS = 4
G_DK = 128
G_DV = 256
G_RANK = 16
G_TAU = 16.0
G_CHUNK = 64
G_BLOCK = 256
G_QK = G_HEADS * G_DK
G_V = G_HEADS * G_DV
ODD_MAIN = 2 * G_QK + 2 * G_V
ODD_PAD = ODD_MAIN + 128

VMEM_LIMIT_BYTES = 56 * 1024 * 1024

BF16 = jnp.bfloat16
F32 = jnp.float32


def _params(*semantics):
    return pltpu.CompilerParams(dimension_semantics=semantics,
                                vmem_limit_bytes=VMEM_LIMIT_BYTES)


def _silu(v):
    return v * (1.0 / (1.0 + jnp.exp(-v)))


def _dot(a, b):
    return jnp.dot(a, b, preferred_element_type=F32)


def _dot_nt(a, b):
    return lax.dot_general(a, b, (((1,), (1,)), ((), ())), preferred_element_type=F32)


def _dot_tn(a, b):
    return lax.dot_general(a, b, (((0,), (0,)), ((), ())), preferred_element_type=F32)


def _split_bf16(v):
    hi = v.astype(BF16)
    lo = (v - hi.astype(F32)).astype(BF16)
    return hi, lo


def _adaln_kernel(cond_ref, w_ref, b_ref, o_ref):
    a = _silu(cond_ref[...])
    a_hi, a_lo = _split_bf16(a)
    w_hi, w_lo = _split_bf16(w_ref[...])
    o_ref[...] = (_dot(a_hi, w_hi) + _dot(a_hi, w_lo) + _dot(a_lo, w_hi)) + b_ref[...]


def _adaln(cond, w, b):
    n_layers = w.shape[0]
    rows = cond.shape[0]
    tn = 512
    return pl.pallas_call(
        _adaln_kernel,
        grid=(n_layers, 3 * D_MODEL // tn),
        in_specs=[
            pl.BlockSpec((rows, D_MODEL), lambda l, n: (0, 0)),
            pl.BlockSpec((None, D_MODEL, tn), lambda l, n: (l, 0, n)),
            pl.BlockSpec((None, 1, tn), lambda l, n: (l, 0, n)),
        ],
        out_specs=pl.BlockSpec((None, rows, tn), lambda l, n: (l, 0, n)),
        out_shape=jax.ShapeDtypeStruct((n_layers, rows, 3 * D_MODEL), F32),
        compiler_params=_params("arbitrary", "arbitrary"),
        name="adaln",
    )(cond, w, b)


def _modulated(x_ref, g_ref, sc_ref, sh_ref):
    xf = x_ref[...]
    r = lax.rsqrt(jnp.mean(xf * xf, axis=-1, keepdims=True) + EPS)
    return ((xf * r) * g_ref[...] * (1.0 + sc_ref[...]) + sh_ref[...]).astype(BF16)


def _mod_spec(per_batch):
    if per_batch:
        return pl.BlockSpec((None, 1, D_MODEL), lambda j, b: (b, 0, 0))
    return pl.BlockSpec((None, 1, D_MODEL), lambda j, b: (0, 0, 0))


def _rope_head(blk, c, s):
    r = lax.rsqrt(jnp.mean(blk * blk, axis=0, keepdims=True) + EPS)
    partner = jnp.concatenate([blk[16:32], blk[0:16], blk[48:64], blk[32:48]], axis=0)
    return (blk * c + partner * s) * r


def _even_proj_kernel(x_ref, g_ref, sc_ref, sh_ref, wt_ref, tab_ref,
                      qa_ref, ka_ref, va_ref, ga_ref, qb_ref, kb_ref, vb_ref, gb_ref):
    h = _modulated(x_ref, g_ref, sc_ref, sh_ref)

    def rows(start, size):
        return _dot_nt(wt_ref[start:start + size, :], h)

    def normed(acc, table):
        c = tab_ref[2 * table]
        s = tab_ref[2 * table + 1]
        n = acc.shape[0] // HEAD_DIM
        return jnp.concatenate(
            [_rope_head(acc[HEAD_DIM * i:HEAD_DIM * (i + 1)], c, s) for i in range(n)], axis=0)

    off = 0
    for i in range(A_WIDTH // 256):
        qa_ref[256 * i:256 * (i + 1), :] = normed(rows(off, 256), 0).astype(BF16)
        off += 256
    ka_ref[...] = normed(rows(off, A_KV_WIDTH), 1).T.astype(BF16)
    off += A_KV_WIDTH
    va_ref[...] = rows(off, A_KV_WIDTH).astype(BF16)
    off += A_KV_WIDTH
    for i in range(A_WIDTH // 256):
        ga_ref[256 * i:256 * (i + 1), :] = rows(off, 256).astype(BF16)
        off += 256
    for i in range(B_WIDTH // 256):
        qb_ref[256 * i:256 * (i + 1), :] = normed(rows(off, 256), 2).astype(BF16)
        off += 256
    for i in range(B_WIDTH // 256):
        kb_ref[:, 256 * i:256 * (i + 1)] = normed(rows(off, 256), 3).T.astype(BF16)
        off += 256
    for i in range(B_WIDTH // 256):
        vb_ref[256 * i:256 * (i + 1), :] = rows(off, 256).astype(BF16)
        off += 256
    for i in range(B_WIDTH // 256):
        gb_ref[256 * i:256 * (i + 1), :] = rows(off, 256).astype(BF16)
        off += 256


def _even_proj(x, norm_g, sc, sh, wt, tables, per_batch, tm):
    bn, t, _ = x.shape
    nt = t // tm

    def fm(width):
        return (pl.BlockSpec((None, width, tm), lambda j, b: (b, 0, j)),
                jax.ShapeDtypeStruct((bn, width, t), BF16))

    def tk(width):
        return (pl.BlockSpec((None, tm, width), lambda j, b: (b, j, 0)),
                jax.ShapeDtypeStruct((bn, t, width), BF16))

    outs = [fm(A_WIDTH), tk(A_KV_WIDTH), fm(A_KV_WIDTH), fm(A_WIDTH),
            fm(B_WIDTH), tk(B_WIDTH), fm(B_WIDTH), fm(B_WIDTH)]
    return pl.pallas_call(
        _even_proj_kernel,
        grid=(nt, bn),
        in_specs=[
            pl.BlockSpec((None, tm, D_MODEL), lambda j, b: (b, j, 0)),
            pl.BlockSpec((1, D_MODEL), lambda j, b: (0, 0)),
            _mod_spec(per_batch), _mod_spec(per_batch),
            pl.BlockSpec((EVEN_IN, D_MODEL), lambda j, b: (0, 0)),
            pl.BlockSpec((8, HEAD_DIM, tm), lambda j, b: (0, 0, j)),
        ],
        out_specs=[o[0] for o in outs],
        out_shape=[o[1] for o in outs],
        compiler_params=_params("arbitrary", "arbitrary"),
        name="even_proj",
    )(x, norm_g, sc, sh, wt, tables)


def _rope_tables(gains, t, rotary, q_scale):
    m = HEAD_DIM // 4
    inv = ROPE_BASE ** (-jnp.arange(m, dtype=F32) / m)
    pos = jnp.arange(t, dtype=jnp.int32)
    rows = (pos // GRID_W).astype(F32)
    cols = (pos % GRID_W).astype(F32)
    if rotary:
        ang_r = inv[:, None] * rows[None, :]
        ang_c = inv[:, None] * cols[None, :]
        cos = jnp.concatenate([jnp.cos(ang_r), jnp.cos(ang_r), jnp.cos(ang_c), jnp.cos(ang_c)], axis=0)
        sin = jnp.concatenate([-jnp.sin(ang_r), jnp.sin(ang_r), -jnp.sin(ang_c), jnp.sin(ang_c)], axis=0)
    else:
        cos = jnp.ones((HEAD_DIM, t), F32)
        sin = jnp.zeros((HEAD_DIM, t), F32)
    out = []
    for g, scale in zip(gains, q_scale):
        g = g.astype(F32)
        gp = jnp.concatenate([g[16:32], g[0:16], g[48:64], g[32:48]])
        out.append(g[:, None] * cos * scale)
        out.append(gp[:, None] * sin * scale)
    return jnp.stack(out)


def _win_attn_kernel(*refs, tq, windowed):
    if windowed:
        (q_ref, kp_ref, kc_ref, kn_ref, kx_ref, vp_ref, vc_ref, vn_ref, vx_ref,
         g_ref, sink_ref, o_ref, qpad_ref) = refs
        segs = [(kp_ref, vp_ref, "prev"), (kc_ref, vc_ref, None), (kn_ref, vn_ref, "next"),
                (kx_ref, vx_ref, None)]
    else:
        q_ref, kx_ref, vx_ref, g_ref, sink_ref, o_ref, qpad_ref = refs
        segs = [(kx_ref, vx_ref, None)]
    width = A_HEADS * tq
    qpad_ref[...] = jnp.zeros_like(qpad_ref)
    for hd in range(A_HEADS):
        grp = hd // A_GROUP
        qpad_ref[HEAD_DIM * grp:HEAD_DIM * (grp + 1), tq * hd:tq * (hd + 1)] = (
            q_ref[HEAD_DIM * hd:HEAD_DIM * (hd + 1), :])
    qpad = qpad_ref[...]

    scores = []
    for k_ref, _, kind in segs:
        s = _dot(k_ref[...], qpad)
        if kind is not None:
            i = pl.program_id(1)
            r = lax.broadcasted_iota(jnp.int32, s.shape, 0)
            cq = lax.broadcasted_iota(jnp.int32, s.shape, 1) % tq
            if kind == "prev":
                ok = r >= cq + jnp.where(i > 0, 0, tq)
            else:
                ok = r <= cq - jnp.where(i < pl.num_programs(1) - 1, 0, tq)
            s = jnp.where(ok, s, NEG_INF)
        scores.append(s)
    sink = sink_ref[...]
    m = sink
    for s in scores:
        m = jnp.maximum(m, jnp.max(s, axis=0, keepdims=True))
    l = jnp.exp2(sink - m)
    acc = jnp.zeros((A_KV_WIDTH, width), F32)
    for s, (_, v_ref, _) in zip(scores, segs):
        p = jnp.exp2(s - m)
        l = l + jnp.sum(p, axis=0, keepdims=True)
        acc = acc + _dot(v_ref[...], p.astype(BF16))
    inv = 1.0 / l
    for hd in range(A_HEADS):
        grp = hd // A_GROUP
        o = acc[HEAD_DIM * grp:HEAD_DIM * (grp + 1), tq * hd:tq * (hd + 1)]
        o = o * inv[:, tq * hd:tq * (hd + 1)]
        gate = g_ref[HEAD_DIM * hd:HEAD_DIM * (hd + 1), :].astype(F32)
        o_ref[HEAD_DIM * hd:HEAD_DIM * (hd + 1), :] = (o.astype(F32) * _silu(gate)).astype(BF16)


def _win_attn(qa, ka, va, kac, vac, ga, sink_row, windowed):
    bn, _, t = qa.shape
    ctx_len = kac.shape[1]
    if windowed:
        tq = Q_BLOCK
        nb = t // tq
        prev = lambda b, i: jnp.maximum(i - 1, 0)
        nxt = lambda b, i: jnp.minimum(i + 1, nb - 1)
        k_specs = [pl.BlockSpec((None, tq, A_KV_WIDTH), lambda b, i: (b, prev(b, i), 0)),
                   pl.BlockSpec((None, tq, A_KV_WIDTH), lambda b, i: (b, i, 0)),
                   pl.BlockSpec((None, tq, A_KV_WIDTH), lambda b, i: (b, nxt(b, i), 0)),
                   pl.BlockSpec((None, ctx_len, A_KV_WIDTH), lambda b, i: (b, 0, 0))]
        v_specs = [pl.BlockSpec((None, A_KV_WIDTH, tq), lambda b, i: (b, 0, prev(b, i))),
                   pl.BlockSpec((None, A_KV_WIDTH, tq), lambda b, i: (b, 0, i)),
                   pl.BlockSpec((None, A_KV_WIDTH, tq), lambda b, i: (b, 0, nxt(b, i))),
                   pl.BlockSpec((None, A_KV_WIDTH, ctx_len), lambda b, i: (b, 0, 0))]
        args = [qa, ka, ka, ka, kac, va, va, va, vac, ga, sink_row]
    else:
        tq = t
        nb = 1
        k_specs = [pl.BlockSpec((None, ctx_len, A_KV_WIDTH), lambda b, i: (b, 0, 0))]
        v_specs = [pl.BlockSpec((None, A_KV_WIDTH, ctx_len), lambda b, i: (b, 0, 0))]
        args = [qa, kac, vac, ga, sink_row]
    q_spec = pl.BlockSpec((None, A_WIDTH, tq), lambda b, i: (b, 0, i))
    return pl.pallas_call(
        functools.partial(_win_attn_kernel, tq=tq, windowed=windowed),
        grid=(bn, nb),
        in_specs=[q_spec] + k_specs + v_specs + [
            q_spec, pl.BlockSpec((1, A_HEADS * tq), lambda b, i: (0, 0))],
        out_specs=q_spec,
        out_shape=jax.ShapeDtypeStruct((bn, A_WIDTH, t), BF16),
        scratch_shapes=[pltpu.VMEM((A_KV_WIDTH, A_HEADS * tq), BF16)],
        compiler_params=_params("arbitrary", "arbitrary"),
        name="win_attn" if windowed else "ctx_sink_attn",
    )(*args)


def _diff_attn_kernel(*refs, tq, key_chunk, with_latent, pipelined):
    if with_latent:
        (q_ref, qn_ref, k_ref, kx_ref, v_ref, vx_ref, g_ref, w_ref, lam_ref, o_ref,
         qpad_a, qpad_b, s_a, s_b, m_a, m_b) = refs
        segs = [(k_ref, v_ref), (kx_ref, vx_ref)]
    else:
        (q_ref, qn_ref, kx_ref, vx_ref, g_ref, w_ref, lam_ref, o_ref,
         qpad_a, qpad_b, s_a, s_b, m_a, m_b) = refs
        segs = [(kx_ref, vx_ref)]
    buf_a = (qpad_a, s_a, m_a)
    buf_b = (qpad_b, s_b, m_b)
    chunks = []
    base = 0
    for k_ref, v_ref in segs:
        n = k_ref.shape[0]
        for c0 in range(0, n, key_chunk):
            size = min(key_chunk, n - c0)
            chunks.append((k_ref, v_ref, c0, size, base + c0))
        base += n

    def scores(q, buf):
        qpad_ref, s_ref, m_ref = buf
        qpad_ref[...] = jnp.zeros_like(qpad_ref)
        qpad_ref[0:HEAD_DIM, 0:tq] = q[0:HEAD_DIM, :]
        qpad_ref[HEAD_DIM:, tq:] = q[HEAD_DIM:, :]
        qpad = qpad_ref[...]
        m = None
        for k_ref, _, c0, size, row in chunks:
            s = _dot(k_ref[c0:c0 + size, :], qpad)
            s_ref[row:row + size, :] = s
            s_max = jnp.max(s, axis=0, keepdims=True)
            m = s_max if m is None else jnp.maximum(m, s_max)
            yield
        m_ref[...] = m

    def attend(buf, col0):
        _, s_ref, m_ref = buf
        m = m_ref[...]
        l = jnp.zeros((1, 2 * tq), F32)
        acc = jnp.zeros((B_VDIM, 2 * tq), F32)
        for _, v_ref, c0, size, row in chunks:
            p = jnp.exp2(s_ref[row:row + size, :] - m)
            l = l + jnp.sum(p, axis=0, keepdims=True)
            acc = acc + _dot(v_ref[:, c0:c0 + size], p.astype(BF16))
            yield
        acc = acc * (1.0 / l)
        o = acc[:, 0:tq] - lam_ref[...] * acc[:, tq:]
        r = lax.rsqrt(jnp.mean(o * o, axis=0, keepdims=True) + EPS)
        o = (o * r) * w_ref[...]
        gate = g_ref[:, col0:col0 + tq].astype(F32)
        o_ref[:, col0:col0 + tq] = (o * _silu(gate)).astype(BF16)

    if not pipelined:
        _interleave(scores(q_ref[...], buf_a))
        _interleave(attend(buf_a, 0))
        return

    @pl.when(pl.program_id(2) == 0)
    def _():
        _interleave(scores(q_ref[:, 0:tq], buf_a))

    _interleave(scores(q_ref[:, tq:], buf_b), attend(buf_a, 0))
    _interleave(scores(qn_ref[...], buf_a), attend(buf_b, tq))


def _diff_attn(qb, kb, vb, kbc, vbc, gb, w_sub, lam_row, tq, with_latent):
    bn, _, t = qb.shape
    ctx_len = kbc.shape[1]
    n_tiles = t // tq
    pipelined = n_tiles % 2 == 0
    per_step = 2 if pipelined else 1
    nq = n_tiles // per_step
    head_fm = lambda width: pl.BlockSpec((None, B_VDIM, width), lambda b, h, j: (b, h, 0))
    q_spec = pl.BlockSpec((None, B_VDIM, per_step * tq), lambda b, h, j: (b, h, j))
    kx_spec = pl.BlockSpec((None, ctx_len, B_VDIM), lambda b, h, j: (b, 0, h))
    q_next_spec = pl.BlockSpec(
        (None, B_VDIM, tq), lambda b, h, j: (b, h, jnp.minimum(per_step * (j + 1), n_tiles - 1)))
    if with_latent:
        t_keys = kb.shape[1]
        in_specs = [q_spec, q_next_spec,
                    pl.BlockSpec((None, t_keys, B_VDIM), lambda b, h, j: (b, 0, h)), kx_spec,
                    head_fm(t_keys), head_fm(ctx_len)]
        args = [qb, qb, kb, kbc, vb, vbc]
        n_keys = t_keys + ctx_len
    else:
        in_specs = [q_spec, q_next_spec, kx_spec, head_fm(ctx_len)]
        args = [qb, qb, kbc, vbc]
        n_keys = ctx_len
    in_specs += [q_spec,
                 pl.BlockSpec((B_VDIM, tq), lambda b, h, j: (0, 0)),
                 pl.BlockSpec((1, tq), lambda b, h, j: (0, 0))]
    args += [gb, w_sub, lam_row]
    return pl.pallas_call(
        functools.partial(_diff_attn_kernel, tq=tq, key_chunk=512, with_latent=with_latent,
                          pipelined=pipelined),
        grid=(bn, B_HEADS, nq),
        in_specs=in_specs,
        out_specs=q_spec,
        out_shape=jax.ShapeDtypeStruct((bn, B_WIDTH, t), BF16),
        scratch_shapes=[pltpu.VMEM((B_VDIM, 2 * tq), BF16), pltpu.VMEM((B_VDIM, 2 * tq), BF16),
                        pltpu.VMEM((n_keys, 2 * tq), F32), pltpu.VMEM((n_keys, 2 * tq), F32),
                        pltpu.VMEM((1, 2 * tq), F32), pltpu.VMEM((1, 2 * tq), F32)],
        compiler_params=_params("arbitrary", "arbitrary", "arbitrary"),
        name="diff_attn" if with_latent else "ctx_diff_attn",
    )(*args)


def _even_out_kernel(oa_ref, ob_ref, w_ref, x_ref, gt_ref, o_ref):
    y = _dot_tn(oa_ref[...], w_ref[0:A_WIDTH, :]) + _dot_tn(ob_ref[...], w_ref[A_WIDTH:, :])
    o_ref[...] = x_ref[...] + gt_ref[...] * y


def _even_out(oa, ob, w, x, gate, per_batch, tm):
    bn, t, _ = x.shape
    fm = pl.BlockSpec((None, A_WIDTH, tm), lambda j, b: (b, 0, j))
    xs = pl.BlockSpec((None, tm, D_MODEL), lambda j, b: (b, j, 0))
    return pl.pallas_call(
        _even_out_kernel,
        grid=(t // tm, bn),
        in_specs=[fm, fm, pl.BlockSpec((D_MODEL, D_MODEL), lambda j, b: (0, 0)), xs,
                  _mod_spec(per_batch)],
        out_specs=xs,
        out_shape=jax.ShapeDtypeStruct(x.shape, F32),
        compiler_params=_params("arbitrary", "arbitrary"),
        name="even_out",
    )(oa, ob, w, x, gate)


def _odd_proj_kernel(x_ref, g_ref, sc_ref, sh_ref, w_ref, wab_ref, bab_ref,
                     q_ref, k_ref, v_ref, gate_ref, cumf_ref, cumb_ref):
    h = _modulated(x_ref, g_ref, sc_ref, sh_ref)

    def cols(start, size):
        return _dot(h, w_ref[:, start:start + size])

    q_ref[...] = (cols(0, G_QK) * (G_DK ** -0.5)).astype(BF16)
    k_ref[...] = cols(G_QK, G_QK).astype(BF16)
    for i in range(G_V // 512):
        v_ref[:, 512 * i:512 * (i + 1)] = cols(2 * G_QK + 512 * i, 512).astype(BF16)
    for i in range(G_V // 512):
        gate_ref[:, 512 * i:512 * (i + 1)] = cols(2 * G_QK + G_V + 512 * i, 512).astype(BF16)
    low = cols(ODD_MAIN, 128).astype(BF16)
    z = _dot(low, wab_ref[...]) + bab_ref[...]
    la = (jnp.minimum(z, 0.0) - jnp.log(1.0 + jnp.exp(-jnp.abs(z)))) / G_TAU
    r = lax.broadcasted_iota(jnp.int32, (G_BLOCK, G_BLOCK), 0)
    c = lax.broadcasted_iota(jnp.int32, (G_BLOCK, G_BLOCK), 1)
    in_chunk = r % G_CHUNK
    tri_f = jnp.where((r - c).astype(jnp.uint32) <= in_chunk.astype(jnp.uint32), 1.0, 0.0)
    tri_b = jnp.where((c - r).astype(jnp.uint32) <= (G_CHUNK - 1 - in_chunk).astype(jnp.uint32),
                      1.0, 0.0)
    for blk in range(la.shape[0] // G_BLOCK):
        rows = slice(blk * G_BLOCK, (blk + 1) * G_BLOCK)
        for tri, lo_col, out_ref in ((tri_f, 0, cumf_ref), (tri_b, G_QK, cumb_ref)):
            hi, lo = _split_bf16(la[rows, lo_col:lo_col + G_QK])
            cum = _dot(tri.astype(BF16), jnp.concatenate([hi, lo], axis=1))
            out_ref[rows, :] = cum[:, :G_QK] + cum[:, G_QK:]


def _odd_proj(x, norm_g, sc, sh, w, wab, bab, per_batch, tm):
    bn, t, _ = x.shape

    def tk(width, dtype):
        return (pl.BlockSpec((None, tm, width), lambda j, b: (b, j, 0)),
                jax.ShapeDtypeStruct((bn, t, width), dtype))

    outs = [tk(G_QK, BF16), tk(G_QK, BF16), tk(G_V, BF16), tk(G_V, BF16), tk(G_QK, F32), tk(G_QK, F32)]
    return pl.pallas_call(
        _odd_proj_kernel,
        grid=(t // tm, bn),
        in_specs=[
            pl.BlockSpec((None, tm, D_MODEL), lambda j, b: (b, j, 0)),
            pl.BlockSpec((1, D_MODEL), lambda j, b: (0, 0)),
            _mod_spec(per_batch), _mod_spec(per_batch),
            pl.BlockSpec((D_MODEL, ODD_PAD), lambda j, b: (0, 0)),
            pl.BlockSpec((128, 2 * G_QK), lambda j, b: (0, 0)),
            pl.BlockSpec((1, 2 * G_QK), lambda j, b: (0, 0)),
        ],
        out_specs=[o[0] for o in outs],
        out_shape=[o[1] for o in outs],
        compiler_params=_params("arbitrary", "arbitrary"),
        name="odd_proj",
    )(x, norm_g, sc, sh, w, wab, bab)


def _interleave(*gens):
    results = [None] * len(gens)
    live = list(range(len(gens)))
    while live:
        for idx in list(live):
            try:
                next(gens[idx])
            except StopIteration as stop:
                results[idx] = stop.value
                live.remove(idx)
    return results


def _gla_block(q, k, v, b, state, reverse, with_out):
    n = G_BLOCK // G_CHUNK
    ends = [(j * G_CHUNK if reverse else (j + 1) * G_CHUNK - 1) for j in range(n)]
    b_last = jnp.concatenate(
        [jnp.broadcast_to(b[e:e + 1, :], (G_CHUNK, G_DK)) for e in ends], axis=0)
    kf = k.astype(F32)
    k_state = (kf * jnp.exp(b_last - b)).astype(BF16)
    if with_out:
        qd = (q.astype(F32) * jnp.exp(b)).astype(BF16)
        kd = (kf * jnp.exp(-b)).astype(BF16)
    yield
    if with_out:
        att = _dot_nt(qd, kd)
        yield
    chunks = [slice(j * G_CHUNK, (j + 1) * G_CHUNK) for j in range(n)]
    d_state = [_dot_tn(k_state[sl], v[sl]) for sl in chunks]
    decay_t = jnp.exp(b_last).T
    yield
    if with_out:
        r = lax.broadcasted_iota(jnp.int32, (G_BLOCK, G_BLOCK), 0)
        c = lax.broadcasted_iota(jnp.int32, (G_BLOCK, G_BLOCK), 1)
        in_chunk = r % G_CHUNK
        if reverse:
            causal = (c - r).astype(jnp.uint32) <= (G_CHUNK - 1 - in_chunk).astype(jnp.uint32)
        else:
            causal = (r - c).astype(jnp.uint32) <= in_chunk.astype(jnp.uint32)
        intra = _dot(jnp.where(causal, att, 0.0).astype(BF16), v)
        yield
    outs = [None] * n
    for j in (reversed(range(n)) if reverse else range(n)):
        if with_out:
            outs[j] = intra[chunks[j]] + _dot(qd[chunks[j]], state.astype(BF16))
        decay = jnp.broadcast_to(decay_t[:, ends[j]:ends[j] + 1], (G_DK, G_DV))
        state = state * decay + d_state[j]
        yield
    return (jnp.concatenate(outs, axis=0) if with_out else None), state


def _gla_kernel(q_ref, k_ref, v_ref, cumf_ref, cumb_ref, g_ref,
                kc_ref, vc_ref, cumfc_ref, cumbc_ref, w_ref, o_ref, of_ref, ob_ref):
    n_lat = q_ref.shape[0] // G_BLOCK
    n_ctx = kc_ref.shape[0] // G_BLOCK

    def rows(i):
        return pl.ds(pl.multiple_of(i * G_BLOCK, G_BLOCK), G_BLOCK)

    st_f = st_b = jnp.zeros((G_DK, G_DV), F32)
    for i in range(n_ctx):
        sl_f = slice(i * G_BLOCK, (i + 1) * G_BLOCK)
        sl_b = slice((n_ctx - 1 - i) * G_BLOCK, (n_ctx - i) * G_BLOCK)
        (_, st_f), (_, st_b) = _interleave(
            _gla_block(None, kc_ref[sl_f, :], vc_ref[sl_f, :], cumfc_ref[sl_f, :], st_f, False, False),
            _gla_block(None, kc_ref[sl_b, :], vc_ref[sl_b, :], cumbc_ref[sl_b, :], st_b, True, False))

    def body(i, states):
        i_b = n_lat - 1 - i
        (o_f, st_f), (o_b, st_b) = _interleave(
            _gla_block(q_ref[rows(i), :], k_ref[rows(i), :], v_ref[rows(i), :],
                       cumf_ref[rows(i), :], states[0], False, True),
            _gla_block(q_ref[rows(i_b), :], k_ref[rows(i_b), :], v_ref[rows(i_b), :],
                       cumb_ref[rows(i_b), :], states[1], True, True))
        of_ref[rows(i), :] = o_f
        ob_ref[rows(i_b), :] = o_b
        return st_f, st_b

    lax.fori_loop(0, n_lat, body, (st_f, st_b))

    def finish(i, carry):
        o = of_ref[rows(i), :] + ob_ref[rows(i), :]
        rn = lax.rsqrt(jnp.mean(o * o, axis=-1, keepdims=True) + EPS)
        o = (o * rn) * w_ref[...]
        o_ref[rows(i), :] = (o * _silu(g_ref[rows(i), :].astype(F32))).astype(BF16)
        return carry

    lax.fori_loop(0, n_lat, finish, 0)


def _gla(q, k, v, laf, lab, gate, kc, vc, lafc, labc, w_norm):
    bn, t, _ = q.shape
    ctx_len = kc.shape[1]
    lat = lambda width: pl.BlockSpec((None, t, width), lambda b, h: (b, 0, h))
    ctx = lambda width: pl.BlockSpec((None, ctx_len, width), lambda b, h: (b, 0, h))
    return pl.pallas_call(
        _gla_kernel,
        grid=(bn, G_HEADS),
        in_specs=[lat(G_DK), lat(G_DK), lat(G_DV), lat(G_DK), lat(G_DK), lat(G_DV),
                  ctx(G_DK), ctx(G_DV), ctx(G_DK), ctx(G_DK),
                  pl.BlockSpec((1, G_DV), lambda b, h: (0, 0))],
        out_specs=lat(G_DV),
        out_shape=jax.ShapeDtypeStruct((bn, t, G_V), BF16),
        scratch_shapes=[pltpu.VMEM((t, G_DV), F32), pltpu.VMEM((t, G_DV), F32)],
        compiler_params=_params("arbitrary", "arbitrary"),
        name="gla",
    )(q, k, v, laf, lab, gate, kc, vc, lafc, labc, w_norm)


def _odd_out_kernel(o_ref, w_ref, x_ref, gt_ref, out_ref):
    out_ref[...] = x_ref[...] + gt_ref[...] * _dot(o_ref[...], w_ref[...])


def _odd_out(o, w, x, gate, tm):
    bn, t, _ = x.shape
    xs = pl.BlockSpec((None, tm, D_MODEL), lambda j, b: (b, j, 0))
    return pl.pallas_call(
        _odd_out_kernel,
        grid=(t // tm, bn),
        in_specs=[xs, pl.BlockSpec((D_MODEL, D_MODEL), lambda j, b: (0, 0)), xs, _mod_spec(True)],
        out_specs=xs,
        out_shape=jax.ShapeDtypeStruct(x.shape, F32),
        compiler_params=_params("arbitrary", "arbitrary"),
        name="odd_out",
    )(o, w, x, gate)


def _token_tile(t):
    return 512 if t % 512 == 0 else t


def kernel(x, c, ctx, c_ctx, adaln_w, adaln_b, norm_g, w_out, ab_w_in, a_q_norm, a_k_norm, a_sink,
           b_q_norm, b_k_norm, b_lambda_q1, b_lambda_k1, b_lambda_q2, b_lambda_k2, b_subln,
           gla_w_in, gla_wa_f, gla_ba_f, gla_wa_b, gla_ba_b, gla_out_norm):
    bn, t, _ = x.shape
    ctx_len = ctx.shape[1]
    depth = adaln_w.shape[0]
    assert depth == 2

    pad_rows = (-(bn + 1)) % 8
    cond = jnp.concatenate([c, c_ctx[None, :], jnp.zeros((pad_rows, D_MODEL), F32)], axis=0)
    mod = _adaln(cond, adaln_w, adaln_b[:, None, :])

    def mods(layer):
        m = mod[layer]
        shift, scale, gate = (m[:, i * D_MODEL:(i + 1) * D_MODEL] for i in range(3))
        per_x = tuple(v[:bn, None, :] for v in (shift, scale, gate))
        per_c = tuple(v[bn:bn + 1, None, :] for v in (shift, scale, gate))
        return per_x, per_c

    (shx, scx, gtx), (shc, scc, gtc) = mods(0)
    lambda_init = 0.8 - 0.6 * math.exp(-0.3 * 0)
    wt = ab_w_in[0].T.astype(BF16)
    gains = [a_q_norm[0], a_k_norm[0], b_q_norm[0], b_k_norm[0]]
    q_scale = [ATTN_SCALE * LOG2E, 1.0, ATTN_SCALE * LOG2E, 1.0]
    tab_x = _rope_tables(gains, t, True, q_scale)
    tab_c = _rope_tables(gains, ctx_len, False, q_scale)
    g0 = norm_g[0][None, :]
    qa, ka, va, ga, qb, kb, vb, gb = _even_proj(x, g0, scx, shx, wt, tab_x, True, _token_tile(t))
    qac, kac, vac, gac, qbc, kbc, vbc, gbc = _even_proj(ctx, g0, scc, shc, wt, tab_c, False,
                                                        _token_tile(ctx_len))

    sink = a_sink[0].astype(F32) * LOG2E
    oa = _win_attn(qa, ka, va, kac, vac, ga, jnp.repeat(sink, Q_BLOCK)[None, :], True)
    oac = _win_attn(qac, None, None, kac, vac, gac, jnp.repeat(sink, ctx_len)[None, :], False)

    lam = (jnp.exp(jnp.sum(b_lambda_q1[0].astype(F32) * b_lambda_k1[0].astype(F32)))
           - jnp.exp(jnp.sum(b_lambda_q2[0].astype(F32) * b_lambda_k2[0].astype(F32))) + lambda_init)
    tq = 256
    w_sub = jnp.broadcast_to((b_subln[0].astype(F32) * (1.0 - lambda_init))[:, None], (B_VDIM, tq))
    lam_row = jnp.broadcast_to(lam, (1, tq)).astype(F32)
    ob = _diff_attn(qb, kb, vb, kbc, vbc, gb, w_sub, lam_row, tq, True)
    obc = _diff_attn(qbc, None, None, kbc, vbc, gbc, w_sub, lam_row, tq, False)

    w0 = w_out[0].astype(BF16)
    x = _even_out(oa, ob, w0, x, gtx, True, _token_tile(t))
    ctx = _even_out(oac, obc, w0, ctx, gtc, False, _token_tile(ctx_len))

    (shx, scx, gtx), (shc, scc, _) = mods(1)
    w1 = jnp.pad(gla_w_in[0], ((0, 0), (0, ODD_PAD - gla_w_in.shape[2]))).astype(BF16)
    wab = jnp.zeros((128, 2 * G_QK), F32)
    wab = wab.at[0:G_RANK, 0:G_QK].set(gla_wa_f[0])
    wab = wab.at[G_RANK:2 * G_RANK, G_QK:].set(gla_wa_b[0]).astype(BF16)
    bab = jnp.concatenate([gla_ba_f[0], gla_ba_b[0]])[None, :].astype(F32)
    g1 = norm_g[1][None, :]
    q, k, v, gate, laf, lab = _odd_proj(x, g1, scx, shx, w1, wab, bab, True, _token_tile(t))
    _, kc, vc, _, lafc, labc = _odd_proj(ctx, g1, scc, shc, w1, wab, bab, False, _token_tile(ctx_len))
    o = _gla(q, k, v, laf, lab, gate, kc, vc, lafc, labc, gla_out_norm[0][None, :].astype(F32))
    return _odd_out(o, w_out[1].astype(BF16), x, gtx, _token_tile(t))
```

```python
import functools
import math

import jax
import jax.numpy as jnp
from jax import lax
from jax.experimental import pallas as pl
from jax.experimental.pallas import tpu as pltpu

D_MODEL = 1024
GRID_W = 64
HEAD_DIM = 64
ROPE_BASE = 10000.0
EPS = 1e-6
NEG_INF = -1e30
ATTN_SCALE = HEAD_DIM ** -0.5
LOG2E = math.log2(math.e)
WINDOW = 128
Q_BLOCK = 128

A_HEADS = 8
A_KV_HEADS = 2
A_GROUP = A_HEADS // A_KV_HEADS
A_WIDTH = A_HEADS * HEAD_DIM
A_KV_WIDTH = A_KV_HEADS * HEAD_DIM
B_HEADS = 4
B_VDIM = 2 * HEAD_DIM
B_WIDTH = B_HEADS * B_VDIM
EVEN_IN = 2 * A_WIDTH + 2 * A_KV_WIDTH + 4 * B_WIDTH

G_HEADS = 4
G_DK = 128
G_DV = 256
G_RANK = 16
G_TAU = 16.0
G_CHUNK = 64
G_BLOCK = 256
G_QK = G_HEADS * G_DK
G_V = G_HEADS * G_DV
ODD_MAIN = 2 * G_QK + 2 * G_V
ODD_PAD = ODD_MAIN + 128

VMEM_LIMIT_BYTES = 56 * 1024 * 1024

BF16 = jnp.bfloat16
F32 = jnp.float32


def _params(*semantics):
    return pltpu.CompilerParams(dimension_semantics=semantics,
                                vmem_limit_bytes=VMEM_LIMIT_BYTES)


def _silu(v):
    return v * (1.0 / (1.0 + jnp.exp(-v)))


def _dot(a, b):
    return jnp.dot(a, b, preferred_element_type=F32)


def _dot_nt(a, b):
    return lax.dot_general(a, b, (((1,), (1,)), ((), ())), preferred_element_type=F32)


def _dot_tn(a, b):
    return lax.dot_general(a, b, (((0,), (0,)), ((), ())), preferred_element_type=F32)


def _split_bf16(v):
    hi = v.astype(BF16)
    lo = (v - hi.astype(F32)).astype(BF16)
    return hi, lo


def _adaln_kernel(cond_ref, w_ref, b_ref, o_ref):
    a = _silu(cond_ref[...])
    a_hi, a_lo = _split_bf16(a)
    w_hi, w_lo = _split_bf16(w_ref[...])
    o_ref[...] = (_dot(a_hi, w_hi) + _dot(a_hi, w_lo) + _dot(a_lo, w_hi)) + b_ref[...]


def _adaln(cond, w, b):
    n_layers = w.shape[0]
    rows = cond.shape[0]
    tn = 512
    return pl.pallas_call(
        _adaln_kernel,
        grid=(n_layers, 3 * D_MODEL // tn),
        in_specs=[
            pl.BlockSpec((rows, D_MODEL), lambda l, n: (0, 0)),
            pl.BlockSpec((None, D_MODEL, tn), lambda l, n: (l, 0, n)),
            pl.BlockSpec((None, 1, tn), lambda l, n: (l, 0, n)),
        ],
        out_specs=pl.BlockSpec((None, rows, tn), lambda l, n: (l, 0, n)),
        out_shape=jax.ShapeDtypeStruct((n_layers, rows, 3 * D_MODEL), F32),
        compiler_params=_params("arbitrary", "arbitrary"),
        name="adaln",
    )(cond, w, b)


def _modulate(xf, g_ref, sc_ref, sh_ref):
    r = lax.rsqrt(jnp.mean(xf * xf, axis=-1, keepdims=True) + EPS)
    return ((xf * r) * g_ref[...] * (1.0 + sc_ref[...]) + sh_ref[...]).astype(BF16)


def _modulated(x_ref, g_ref, sc_ref, sh_ref):
    return _modulate(x_ref[...], g_ref, sc_ref, sh_ref)


def _interleave(*gens):
    results = [None] * len(gens)
    live = list(range(len(gens)))
    while live:
        for idx in list(live):
            try:
                next(gens[idx])
            except StopIteration as stop:
                results[idx] = stop.value
                live.remove(idx)
    return results


def _mod_spec(per_batch):
    if per_batch:
        return pl.BlockSpec((None, 1, D_MODEL), lambda j, b: (b, 0, 0))
    return pl.BlockSpec((None, 1, D_MODEL), lambda j, b: (0, 0, 0))


def _rope_head(blk, c, s):
    r = lax.rsqrt(jnp.mean(blk * blk, axis=0, keepdims=True) + EPS)
    partner = jnp.concatenate([blk[16:32], blk[0:16], blk[48:64], blk[32:48]], axis=0)
    return (blk * c + partner * s) * r


def _even_proj_kernel(x_ref, g_ref, sc_ref, sh_ref, wt_ref, tab_ref,
                      qa_ref, ka_ref, va_ref, ga_ref, qb_ref, kb_ref, vb_ref, gb_ref):
    h = _modulated(x_ref, g_ref, sc_ref, sh_ref)

    def rows(start, size):
        return _dot_nt(wt_ref[start:start + size, :], h)

    def normed(acc, table):
        c = tab_ref[2 * table]
        s = tab_ref[2 * table + 1]
        n = acc.shape[0] // HEAD_DIM
        return jnp.concatenate(
            [_rope_head(acc[HEAD_DIM * i:HEAD_DIM * (i + 1)], c, s) for i in range(n)], axis=0)

    off = 0
    qa_ref[...] = normed(rows(off, A_WIDTH), 0).astype(BF16)
    off += A_WIDTH
    kv = rows(off, 2 * A_KV_WIDTH)
    ka_ref[...] = normed(kv[0:A_KV_WIDTH], 1).T.astype(BF16)
    va_ref[...] = kv[A_KV_WIDTH:].astype(BF16)
    off += 2 * A_KV_WIDTH
    ga_ref[...] = rows(off, A_WIDTH).astype(BF16)
    off += A_WIDTH
    qb_ref[...] = normed(rows(off, B_WIDTH), 2).astype(BF16)
    off += B_WIDTH
    kb_ref[...] = normed(rows(off, B_WIDTH), 3).T.astype(BF16)
    off += B_WIDTH
    vb_ref[...] = rows(off, B_WIDTH).astype(BF16)
    off += B_WIDTH
    gb_ref[...] = rows(off, B_WIDTH).astype(BF16)


def _even_proj(x, norm_g, sc, sh, wt, tables, per_batch, tm):
    bn, t, _ = x.shape
    nt = t // tm

    def fm(width):
        return (pl.BlockSpec((None, width, tm), lambda j, b: (b, 0, j)),
                jax.ShapeDtypeStruct((bn, width, t), BF16))

    def tk(width):
        return (pl.BlockSpec((None, tm, width), lambda j, b: (b, j, 0)),
                jax.ShapeDtypeStruct((bn, t, width), BF16))

    outs = [fm(A_WIDTH), tk(A_KV_WIDTH), fm(A_KV_WIDTH), fm(A_WIDTH),
            fm(B_WIDTH), tk(B_WIDTH), fm(B_WIDTH), fm(B_WIDTH)]
    return pl.pallas_call(
        _even_proj_kernel,
        grid=(nt, bn),
        in_specs=[
            pl.BlockSpec((None, tm, D_MODEL), lambda j, b: (b, j, 0)),
            pl.BlockSpec((1, D_MODEL), lambda j, b: (0, 0)),
            _mod_spec(per_batch), _mod_spec(per_batch),
            pl.BlockSpec((EVEN_IN, D_MODEL), lambda j, b: (0, 0)),
            pl.BlockSpec((8, HEAD_DIM, tm), lambda j, b: (0, 0, j)),
        ],
        out_specs=[o[0] for o in outs],
        out_shape=[o[1] for o in outs],
        compiler_params=_params("arbitrary", "arbitrary"),
        name="even_proj",
    )(x, norm_g, sc, sh, wt, tables)


def _rope_tables(gains, t, rotary, q_scale):
    m = HEAD_DIM // 4
    inv = ROPE_BASE ** (-jnp.arange(m, dtype=F32) / m)
    pos = jnp.arange(t, dtype=jnp.int32)
    rows = (pos // GRID_W).astype(F32)
    cols = (pos % GRID_W).astype(F32)
    if rotary:
        ang_r = inv[:, None] * rows[None, :]
        ang_c = inv[:, None] * cols[None, :]
        cos = jnp.concatenate([jnp.cos(ang_r), jnp.cos(ang_r), jnp.cos(ang_c), jnp.cos(ang_c)], axis=0)
        sin = jnp.concatenate([-jnp.sin(ang_r), jnp.sin(ang_r), -jnp.sin(ang_c), jnp.sin(ang_c)], axis=0)
    else:
        cos = jnp.ones((HEAD_DIM, t), F32)
        sin = jnp.zeros((HEAD_DIM, t), F32)
    out = []
    for g, scale in zip(gains, q_scale):
        g = g.astype(F32)
        gp = jnp.concatenate([g[16:32], g[0:16], g[48:64], g[32:48]])
        out.append(g[:, None] * cos * scale)
        out.append(gp[:, None] * sin * scale)
    return jnp.stack(out)


def _win_attn_kernel(*refs, tq, windowed):
    width = A_HEADS * tq
    if windowed:
        (q_ref, qn_ref, kp_ref, kc_ref, kn_ref, kx_ref, vp_ref, vc_ref, vn_ref, vx_ref,
         g_ref, sink_ref, o_ref, qpad_a, qpad_b, s_a, s_b, m_a, m_b) = refs
    else:
        (q_ref, kx_ref, vx_ref, g_ref, sink_ref, o_ref, qpad_a, qpad_b, s_a, s_b, m_a, m_b) = refs
    buf_a = (qpad_a, s_a, m_a)
    buf_b = (qpad_b, s_b, m_b)

    def scores(load_q, key_segs, buf):
        qpad_ref, s_ref, m_ref = buf
        q = load_q()
        qpad_ref[...] = jnp.zeros_like(qpad_ref)
        for hd in range(A_HEADS):
            grp = hd // A_GROUP
            qpad_ref[HEAD_DIM * grp:HEAD_DIM * (grp + 1), tq * hd:tq * (hd + 1)] = (
                q[HEAD_DIM * hd:HEAD_DIM * (hd + 1), :])
        qpad = qpad_ref[...]
        m = sink_ref[...]
        row = 0
        for load_key, kind, valid in key_segs:
            s = _dot(load_key(), qpad)
            n = s.shape[0]
            if kind is not None:
                r = lax.broadcasted_iota(jnp.int32, s.shape, 0)
                cq = lax.broadcasted_iota(jnp.int32, s.shape, 1) % tq
                off = 0 if valid is None else jnp.where(valid, 0, tq)
                ok = (r >= cq + off) if kind == "prev" else (r <= cq - off)
                s = jnp.where(ok, s, NEG_INF)
            s_ref[row:row + n, :] = s
            m = jnp.maximum(m, jnp.max(s, axis=0, keepdims=True))
            row += n
            yield
        m_ref[...] = m

    def attend(value_segs, buf, col0):
        _, s_ref, m_ref = buf
        m = m_ref[...]
        l = jnp.exp2(sink_ref[...] - m)
        acc = jnp.zeros((A_KV_WIDTH, width), F32)
        row = 0
        for load_value in value_segs:
            v = load_value()
            n = v.shape[1]
            p = jnp.exp2(s_ref[row:row + n, :] - m)
            l = l + jnp.sum(p, axis=0, keepdims=True)
            acc = acc + _dot(v, p.astype(BF16))
            row += n
            yield
        inv = 1.0 / l
        for hd in range(A_HEADS):
            grp = hd // A_GROUP
            o = acc[HEAD_DIM * grp:HEAD_DIM * (grp + 1), tq * hd:tq * (hd + 1)]
            o = o * inv[:, tq * hd:tq * (hd + 1)]
            gate = g_ref[HEAD_DIM * hd:HEAD_DIM * (hd + 1), col0:col0 + tq].astype(F32)
            o_ref[HEAD_DIM * hd:HEAD_DIM * (hd + 1), col0:col0 + tq] = (o * _silu(gate)).astype(BF16)

    ctx_keys = (lambda: kx_ref[...], None, None)
    ctx_vals = lambda: vx_ref[...]
    if not windowed:
        _interleave(scores(lambda: q_ref[...], [ctx_keys], buf_a))
        _interleave(attend([ctx_vals], buf_a, 0))
        return

    j = pl.program_id(1)
    last = pl.num_programs(1) - 1
    cur_lo = lambda: kc_ref[0:tq, :]
    cur_hi = lambda: kc_ref[tq:, :]
    nxt_lo = lambda: kn_ref[0:tq, :]
    nxt_hi = lambda: kn_ref[tq:, :]
    x_keys = [(lambda: kp_ref[...], "prev", j > 0), (cur_lo, None, None), (cur_hi, "next", None), ctx_keys]
    y_keys = [(cur_lo, "prev", None), (cur_hi, None, None), (nxt_lo, "next", j < last), ctx_keys]
    z_keys = [(cur_hi, "prev", None), (nxt_lo, None, None), (nxt_hi, "next", None), ctx_keys]
    x_vals = [lambda: vp_ref[...], lambda: vc_ref[:, 0:tq], lambda: vc_ref[:, tq:], ctx_vals]
    y_vals = [lambda: vc_ref[:, 0:tq], lambda: vc_ref[:, tq:], lambda: vn_ref[:, 0:tq], ctx_vals]

    @pl.when(j == 0)
    def _():
        _interleave(scores(lambda: q_ref[:, 0:tq], x_keys, buf_a))

    _interleave(scores(lambda: q_ref[:, tq:], y_keys, buf_b), attend(x_vals, buf_a, 0))
    _interleave(scores(lambda: qn_ref[...], z_keys, buf_a), attend(y_vals, buf_b, tq))


def _win_attn(qa, ka, va, kac, vac, ga, sink_row, windowed):
    bn, _, t = qa.shape
    ctx_len = kac.shape[1]
    kx_spec = pl.BlockSpec((None, ctx_len, A_KV_WIDTH), lambda b, j: (b, 0, 0))
    vx_spec = pl.BlockSpec((None, A_KV_WIDTH, ctx_len), lambda b, j: (b, 0, 0))
    if windowed:
        tq = Q_BLOCK
        nb = t // tq
        assert nb % 2 == 0
        steps = nb // 2
        n_keys = 3 * tq + ctx_len
        before = lambda j: jnp.maximum(2 * j - 1, 0)
        after = lambda j: jnp.minimum(j + 1, steps - 1)
        pair_spec = pl.BlockSpec((None, A_WIDTH, 2 * tq), lambda b, j: (b, 0, j))
        in_specs = [pair_spec,
                    pl.BlockSpec((None, A_WIDTH, tq), lambda b, j: (b, 0, jnp.minimum(2 * j + 2, nb - 1))),
                    pl.BlockSpec((None, tq, A_KV_WIDTH), lambda b, j: (b, before(j), 0)),
                    pl.BlockSpec((None, 2 * tq, A_KV_WIDTH), lambda b, j: (b, j, 0)),
                    pl.BlockSpec((None, 2 * tq, A_KV_WIDTH), lambda b, j: (b, after(j), 0)),
                    kx_spec,
                    pl.BlockSpec((None, A_KV_WIDTH, tq), lambda b, j: (b, 0, before(j))),
                    pl.BlockSpec((None, A_KV_WIDTH, 2 * tq), lambda b, j: (b, 0, j)),
                    pl.BlockSpec((None, A_KV_WIDTH, 2 * tq), lambda b, j: (b, 0, after(j))),
                    vx_spec, pair_spec]
        args = [qa, qa, ka, ka, ka, kac, va, va, va, vac, ga, sink_row]
        out_spec = pair_spec
    else:
        tq = t
        steps = 1
        n_keys = ctx_len
        out_spec = pl.BlockSpec((None, A_WIDTH, tq), lambda b, j: (b, 0, 0))
        in_specs = [out_spec, kx_spec, vx_spec, out_spec]
        args = [qa, kac, vac, ga, sink_row]
    width = A_HEADS * tq
    return pl.pallas_call(
        functools.partial(_win_attn_kernel, tq=tq, windowed=windowed),
        grid=(bn, steps),
        in_specs=in_specs + [pl.BlockSpec((1, width), lambda b, j: (0, 0))],
        out_specs=out_spec,
        out_shape=jax.ShapeDtypeStruct((bn, A_WIDTH, t), BF16),
        scratch_shapes=[pltpu.VMEM((A_KV_WIDTH, width), BF16), pltpu.VMEM((A_KV_WIDTH, width), BF16),
                        pltpu.VMEM((n_keys, width), F32), pltpu.VMEM((n_keys, width), F32),
                        pltpu.VMEM((1, width), F32), pltpu.VMEM((1, width), F32)],
        compiler_params=_params("arbitrary", "arbitrary"),
        name="win_attn" if windowed else "ctx_sink_attn",
    )(*args)


def _diff_attn_kernel(*refs, tq, key_chunk, with_latent, pipelined):
    if with_latent:
        (q_ref, qn_ref, k_ref, kx_ref, v_ref, vx_ref, g_ref, w_ref, lam_ref, o_ref,
         qpad_a, qpad_b, s_a, s_b, m_a, m_b) = refs
        segs = [(k_ref, v_ref), (kx_ref, vx_ref)]
    else:
        (q_ref, qn_ref, kx_ref, vx_ref, g_ref, w_ref, lam_ref, o_ref,
         qpad_a, qpad_b, s_a, s_b, m_a, m_b) = refs
        segs = [(kx_ref, vx_ref)]
    buf_a = (qpad_a, s_a, m_a)
    buf_b = (qpad_b, s_b, m_b)
    chunks = []
    base = 0
    for k_ref, v_ref in segs:
        n = k_ref.shape[0]
        for c0 in range(0, n, key_chunk):
            size = min(key_chunk, n - c0)
            chunks.append((k_ref, v_ref, c0, size, base + c0))
        base += n

    def scores(q, buf):
        qpad_ref, s_ref, m_ref = buf
        qpad_ref[...] = jnp.zeros_like(qpad_ref)
        qpad_ref[0:HEAD_DIM, 0:tq] = q[0:HEAD_DIM, :]
        qpad_ref[HEAD_DIM:, tq:] = q[HEAD_DIM:, :]
        qpad = qpad_ref[...]
        m = None
        for k_ref, _, c0, size, row in chunks:
            s = _dot(k_ref[c0:c0 + size, :], qpad)
            s_ref[row:row + size, :] = s
            s_max = jnp.max(s, axis=0, keepdims=True)
            m = s_max if m is None else jnp.maximum(m, s_max)
            yield
        m_ref[...] = m

    def attend(buf, col0):
        _, s_ref, m_ref = buf
        m = m_ref[...]
        l = jnp.zeros((1, 2 * tq), F32)
        acc = jnp.zeros((B_VDIM, 2 * tq), F32)
        for _, v_ref, c0, size, row in chunks:
            p = jnp.exp2(s_ref[row:row + size, :] - m)
            l = l + jnp.sum(p, axis=0, keepdims=True)
            acc = acc + _dot(v_ref[:, c0:c0 + size], p.astype(BF16))
            yield
        acc = acc * (1.0 / l)
        o = acc[:, 0:tq] - lam_ref[...] * acc[:, tq:]
        r = lax.rsqrt(jnp.mean(o * o, axis=0, keepdims=True) + EPS)
        o = (o * r) * w_ref[...]
        gate = g_ref[:, col0:col0 + tq].astype(F32)
        o_ref[:, col0:col0 + tq] = (o * _silu(gate)).astype(BF16)

    if not pipelined:
        _interleave(scores(q_ref[...], buf_a))
        _interleave(attend(buf_a, 0))
        return

    @pl.when(pl.program_id(2) == 0)
    def _():
        _interleave(scores(q_ref[:, 0:tq], buf_a))

    _interleave(scores(q_ref[:, tq:], buf_b), attend(buf_a, 0))
    _interleave(scores(qn_ref[...], buf_a), attend(buf_b, tq))


def _diff_attn(qb, kb, vb, kbc, vbc, gb, w_sub, lam_row, tq, with_latent):
    bn, _, t = qb.shape
    ctx_len = kbc.shape[1]
    n_tiles = t // tq
    pipelined = n_tiles % 2 == 0
    per_step = 2 if pipelined else 1
    nq = n_tiles // per_step
    head_fm = lambda width: pl.BlockSpec((None, B_VDIM, width), lambda b, h, j: (b, h, 0))
    q_spec = pl.BlockSpec((None, B_VDIM, per_step * tq), lambda b, h, j: (b, h, j))
    kx_spec = pl.BlockSpec((None, ctx_len, B_VDIM), lambda b, h, j: (b, 0, h))
    q_next_spec = pl.BlockSpec(
        (None, B_VDIM, tq), lambda b, h, j: (b, h, jnp.minimum(per_step * (j + 1), n_tiles - 1)))
    if with_latent:
        t_keys = kb.shape[1]
        in_specs = [q_spec, q_next_spec,
                    pl.BlockSpec((None, t_keys, B_VDIM), lambda b, h, j: (b, 0, h)), kx_spec,
                    head_fm(t_keys), head_fm(ctx_len)]
        args = [qb, qb, kb, kbc, vb, vbc]
        n_keys = t_keys + ctx_len
    else:
        in_specs = [q_spec, q_next_spec, kx_spec, head_fm(ctx_len)]
        args = [qb, qb, kbc, vbc]
        n_keys = ctx_len
    in_specs += [q_spec,
                 pl.BlockSpec((B_VDIM, tq), lambda b, h, j: (0, 0)),
                 pl.BlockSpec((1, tq), lambda b, h, j: (0, 0))]
    args += [gb, w_sub, lam_row]
    return pl.pallas_call(
        functools.partial(_diff_attn_kernel, tq=tq, key_chunk=512, with_latent=with_latent,
                          pipelined=pipelined),
        grid=(bn, B_HEADS, nq),
        in_specs=in_specs,
        out_specs=q_spec,
        out_shape=jax.ShapeDtypeStruct((bn, B_WIDTH, t), BF16),
        scratch_shapes=[pltpu.VMEM((B_VDIM, 2 * tq), BF16), pltpu.VMEM((B_VDIM, 2 * tq), BF16),
                        pltpu.VMEM((n_keys, 2 * tq), F32), pltpu.VMEM((n_keys, 2 * tq), F32),
                        pltpu.VMEM((1, 2 * tq), F32), pltpu.VMEM((1, 2 * tq), F32)],
        compiler_params=_params("arbitrary", "arbitrary", "arbitrary"),
        name="diff_attn" if with_latent else "ctx_diff_attn",
    )(*args)


def _out_proj_kernel(oa_ref, ob_ref, w0_ref, x_ref, gt_ref, g_ref, sc_ref, sh_ref,
                     w_ref, wab_ref, bab_ref,
                     xo_ref, q_ref, k_ref, v_ref, gate_ref, cumf_ref, cumb_ref):
    y = _dot_tn(oa_ref[...], w0_ref[0:A_WIDTH, :]) + _dot_tn(ob_ref[...], w0_ref[A_WIDTH:, :])
    x_new = x_ref[...] + gt_ref[...] * y
    xo_ref[...] = x_new
    h = _modulate(x_new, g_ref, sc_ref, sh_ref)

    def cols(start, size):
        return _dot(h, w_ref[:, start:start + size])

    q_ref[...] = (cols(0, G_QK) * (G_DK ** -0.5)).astype(BF16)
    k_ref[...] = cols(G_QK, G_QK).astype(BF16)
    for i in range(G_V // 512):
        v_ref[:, 512 * i:512 * (i + 1)] = cols(2 * G_QK + 512 * i, 512).astype(BF16)
    for i in range(G_V // 512):
        gate_ref[:, 512 * i:512 * (i + 1)] = cols(2 * G_QK + G_V + 512 * i, 512).astype(BF16)
    low = cols(ODD_MAIN, 128).astype(BF16)
    z = _dot(low, wab_ref[...]) + bab_ref[...]
    la = (jnp.minimum(z, 0.0) - jnp.log(1.0 + jnp.exp(-jnp.abs(z)))) / G_TAU
    r = lax.broadcasted_iota(jnp.int32, (G_BLOCK, G_BLOCK), 0)
    c = lax.broadcasted_iota(jnp.int32, (G_BLOCK, G_BLOCK), 1)
    in_chunk = r % G_CHUNK
    tri_f = jnp.where((r - c).astype(jnp.uint32) <= in_chunk.astype(jnp.uint32), 1.0, 0.0)
    tri_b = jnp.where((c - r).astype(jnp.uint32) <= (G_CHUNK - 1 - in_chunk).astype(jnp.uint32),
                      1.0, 0.0)
    for blk in range(la.shape[0] // G_BLOCK):
        rows = slice(blk * G_BLOCK, (blk + 1) * G_BLOCK)
        for tri, lo_col, out_ref in ((tri_f, 0, cumf_ref), (tri_b, G_QK, cumb_ref)):
            hi, lo = _split_bf16(la[rows, lo_col:lo_col + G_QK])
            cum = _dot(tri.astype(BF16), jnp.concatenate([hi, lo], axis=1))
            out_ref[rows, :] = cum[:, :G_QK] + cum[:, G_QK:]


def _out_proj(oa, ob, w0, x, gate, norm_g, sc, sh, w, wab, bab, per_batch, tm):
    bn, t, _ = x.shape
    fm = pl.BlockSpec((None, A_WIDTH, tm), lambda j, b: (b, 0, j))

    def tk(width, dtype):
        return (pl.BlockSpec((None, tm, width), lambda j, b: (b, j, 0)),
                jax.ShapeDtypeStruct((bn, t, width), dtype))

    const = lambda rows, cols: pl.BlockSpec((rows, cols), lambda j, b: (0, 0))
    outs = [tk(D_MODEL, F32), tk(G_QK, BF16), tk(G_QK, BF16), tk(G_V, BF16), tk(G_V, BF16),
            tk(G_QK, F32), tk(G_QK, F32)]
    return pl.pallas_call(
        _out_proj_kernel,
        grid=(t // tm, bn),
        in_specs=[
            fm, fm, const(D_MODEL, D_MODEL), outs[0][0], _mod_spec(per_batch),
            const(1, D_MODEL), _mod_spec(per_batch), _mod_spec(per_batch),
            const(D_MODEL, ODD_PAD), const(128, 2 * G_QK), const(1, 2 * G_QK),
        ],
        out_specs=[o[0] for o in outs],
        out_shape=[o[1] for o in outs],
        compiler_params=_params("arbitrary", "arbitrary"),
        name="out_proj",
    )(oa, ob, w0, x, gate, norm_g, sc, sh, w, wab, bab)


def _gla_block(q, k, v, b, state, reverse, with_out):
    n = G_BLOCK // G_CHUNK
    ends = [(j * G_CHUNK if reverse else (j + 1) * G_CHUNK - 1) for j in range(n)]
    b_last = jnp.concatenate(
        [jnp.broadcast_to(b[e:e + 1, :], (G_CHUNK, G_DK)) for e in ends], axis=0)
    kf = k.astype(F32)
    k_state = (kf * jnp.exp(b_last - b)).astype(BF16)
    if with_out:
        qd = (q.astype(F32) * jnp.exp(b)).astype(BF16)
        kd = (kf * jnp.exp(-b)).astype(BF16)
    yield
    if with_out:
        att = _dot_nt(qd, kd)
        yield
    chunks = [slice(j * G_CHUNK, (j + 1) * G_CHUNK) for j in range(n)]
    d_state = [_dot_tn(k_state[sl], v[sl]) for sl in chunks]
    decay_t = jnp.exp(b_last).T
    yield
    if with_out:
        r = lax.broadcasted_iota(jnp.int32, (G_BLOCK, G_BLOCK), 0)
        c = lax.broadcasted_iota(jnp.int32, (G_BLOCK, G_BLOCK), 1)
        in_chunk = r % G_CHUNK
        if reverse:
            causal = (c - r).astype(jnp.uint32) <= (G_CHUNK - 1 - in_chunk).astype(jnp.uint32)
        else:
            causal = (r - c).astype(jnp.uint32) <= in_chunk.astype(jnp.uint32)
        intra = _dot(jnp.where(causal, att, 0.0).astype(BF16), v)
        yield
    outs = [None] * n
    for j in (reversed(range(n)) if reverse else range(n)):
        if with_out:
            outs[j] = intra[chunks[j]] + _dot(qd[chunks[j]], state.astype(BF16))
        decay = jnp.broadcast_to(decay_t[:, ends[j]:ends[j] + 1], (G_DK, G_DV))
        state = state * decay + d_state[j]
        yield
    return (jnp.concatenate(outs, axis=0) if with_out else None), state


def _gla_kernel(q_ref, k_ref, v_ref, cumf_ref, cumb_ref, g_ref,
                kc_ref, vc_ref, cumfc_ref, cumbc_ref, w_ref, o_ref, of_ref, ob_ref):
    n_lat = q_ref.shape[0] // G_BLOCK
    n_ctx = kc_ref.shape[0] // G_BLOCK

    def rows(i):
        return pl.ds(pl.multiple_of(i * G_BLOCK, G_BLOCK), G_BLOCK)

    st_f = st_b = jnp.zeros((G_DK, G_DV), F32)
    for i in range(n_ctx):
        sl_f = slice(i * G_BLOCK, (i + 1) * G_BLOCK)
        sl_b = slice((n_ctx - 1 - i) * G_BLOCK, (n_ctx - i) * G_BLOCK)
        (_, st_f), (_, st_b) = _interleave(
            _gla_block(None, kc_ref[sl_f, :], vc_ref[sl_f, :], cumfc_ref[sl_f, :], st_f, False, False),
            _gla_block(None, kc_ref[sl_b, :], vc_ref[sl_b, :], cumbc_ref[sl_b, :], st_b, True, False))

    def body(i, states):
        i_b = n_lat - 1 - i
        (o_f, st_f), (o_b, st_b) = _interleave(
            _gla_block(q_ref[rows(i), :], k_ref[rows(i), :], v_ref[rows(i), :],
                       cumf_ref[rows(i), :], states[0], False, True),
            _gla_block(q_ref[rows(i_b), :], k_ref[rows(i_b), :], v_ref[rows(i_b), :],
                       cumb_ref[rows(i_b), :], states[1], True, True))
        of_ref[rows(i), :] = o_f
        ob_ref[rows(i_b), :] = o_b
        return st_f, st_b

    lax.fori_loop(0, n_lat, body, (st_f, st_b), unroll=2)

    def finish(i, carry):
        o = of_ref[rows(i), :] + ob_ref[rows(i), :]
        rn = lax.rsqrt(jnp.mean(o * o, axis=-1, keepdims=True) + EPS)
        o = (o * rn) * w_ref[...]
        o_ref[rows(i), :] = (o * _silu(g_ref[rows(i), :].astype(F32))).astype(BF16)
        return carry

    lax.fori_loop(0, n_lat, finish, 0)


def _gla(q, k, v, laf, lab, gate, kc, vc, lafc, labc, w_norm):
    bn, t, _ = q.shape
    ctx_len = kc.shape[1]
    lat = lambda width: pl.BlockSpec((None, t, width), lambda b, h: (b, 0, h))
    ctx = lambda width: pl.BlockSpec((None, ctx_len, width), lambda b, h: (b, 0, h))
    return pl.pallas_call(
        _gla_kernel,
        grid=(bn, G_HEADS),
        in_specs=[lat(G_DK), lat(G_DK), lat(G_DV), lat(G_DK), lat(G_DK), lat(G_DV),
                  ctx(G_DK), ctx(G_DV), ctx(G_DK), ctx(G_DK),
                  pl.BlockSpec((1, G_DV), lambda b, h: (0, 0))],
        out_specs=lat(G_DV),
        out_shape=jax.ShapeDtypeStruct((bn, t, G_V), BF16),
        scratch_shapes=[pltpu.VMEM((t, G_DV), F32), pltpu.VMEM((t, G_DV), F32)],
        compiler_params=_params("arbitrary", "arbitrary"),
        name="gla",
    )(q, k, v, laf, lab, gate, kc, vc, lafc, labc, w_norm)


def _odd_out_kernel(o_ref, w_ref, x_ref, gt_ref, out_ref):
    out_ref[...] = x_ref[...] + gt_ref[...] * _dot(o_ref[...], w_ref[...])


def _odd_out(o, w, x, gate, tm):
    bn, t, _ = x.shape
    xs = pl.BlockSpec((None, tm, D_MODEL), lambda j, b: (b, j, 0))
    return pl.pallas_call(
        _odd_out_kernel,
        grid=(t // tm, bn),
        in_specs=[xs, pl.BlockSpec((D_MODEL, D_MODEL), lambda j, b: (0, 0)), xs, _mod_spec(True)],
        out_specs=xs,
        out_shape=jax.ShapeDtypeStruct(x.shape, F32),
        compiler_params=_params("arbitrary", "arbitrary"),
        name="odd_out",
    )(o, w, x, gate)


def _token_tile(t):
    return 512 if t % 512 == 0 else t


def kernel(x, c, ctx, c_ctx, adaln_w, adaln_b, norm_g, w_out, ab_w_in, a_q_norm, a_k_norm, a_sink,
           b_q_norm, b_k_norm, b_lambda_q1, b_lambda_k1, b_lambda_q2, b_lambda_k2, b_subln,
           gla_w_in, gla_wa_f, gla_ba_f, gla_wa_b, gla_ba_b, gla_out_norm):
    bn, t, _ = x.shape
    ctx_len = ctx.shape[1]
    depth = adaln_w.shape[0]
    assert depth == 2

    pad_rows = (-(bn + 1)) % 8
    cond = jnp.concatenate([c, c_ctx[None, :], jnp.zeros((pad_rows, D_MODEL), F32)], axis=0)
    mod = _adaln(cond, adaln_w, adaln_b[:, None, :])

    def mods(layer):
        m = mod[layer]
        shift, scale, gate = (m[:, i * D_MODEL:(i + 1) * D_MODEL] for i in range(3))
        per_x = tuple(v[:bn, None, :] for v in (shift, scale, gate))
        per_c = tuple(v[bn:bn + 1, None, :] for v in (shift, scale, gate))
        return per_x, per_c

    (shx, scx, gtx), (shc, scc, gtc) = mods(0)
    lambda_init = 0.8 - 0.6 * math.exp(-0.3 * 0)
    wt = ab_w_in[0].T.astype(BF16)
    gains = [a_q_norm[0], a_k_norm[0], b_q_norm[0], b_k_norm[0]]
    q_scale = [ATTN_SCALE * LOG2E, 1.0, ATTN_SCALE * LOG2E, 1.0]
    tab_x = _rope_tables(gains, t, True, q_scale)
    tab_c = _rope_tables(gains, ctx_len, False, q_scale)
    g0 = norm_g[0][None, :]
    qa, ka, va, ga, qb, kb, vb, gb = _even_proj(x, g0, scx, shx, wt, tab_x, True, _token_tile(t))
    qac, kac, vac, gac, qbc, kbc, vbc, gbc = _even_proj(ctx, g0, scc, shc, wt, tab_c, False,
                                                        _token_tile(ctx_len))

    sink = a_sink[0].astype(F32) * LOG2E
    oa = _win_attn(qa, ka, va, kac, vac, ga, jnp.repeat(sink, Q_BLOCK)[None, :], True)
    oac = _win_attn(qac, None, None, kac, vac, gac, jnp.repeat(sink, ctx_len)[None, :], False)

    lam = (jnp.exp(jnp.sum(b_lambda_q1[0].astype(F32) * b_lambda_k1[0].astype(F32)))
           - jnp.exp(jnp.sum(b_lambda_q2[0].astype(F32) * b_lambda_k2[0].astype(F32))) + lambda_init)
    tq = 256
    w_sub = jnp.broadcast_to((b_subln[0].astype(F32) * (1.0 - lambda_init))[:, None], (B_VDIM, tq))
    lam_row = jnp.broadcast_to(lam, (1, tq)).astype(F32)
    ob = _diff_attn(qb, kb, vb, kbc, vbc, gb, w_sub, lam_row, tq, True)
    obc = _diff_attn(qbc, None, None, kbc, vbc, gbc, w_sub, lam_row, tq, False)

    w0 = w_out[0].astype(BF16)
    (shx1, scx1, gtx1), (shc1, scc1, _) = mods(1)
    w1 = jnp.pad(gla_w_in[0], ((0, 0), (0, ODD_PAD - gla_w_in.shape[2]))).astype(BF16)
    wab = jnp.zeros((128, 2 * G_QK), F32)
    wab = wab.at[0:G_RANK, 0:G_QK].set(gla_wa_f[0])
    wab = wab.at[G_RANK:2 * G_RANK, G_QK:].set(gla_wa_b[0]).astype(BF16)
    bab = jnp.concatenate([gla_ba_f[0], gla_ba_b[0]])[None, :].astype(F32)
    g1 = norm_g[1][None, :]
    x, q, k, v, gate, cumf, cumb = _out_proj(oa, ob, w0, x, gtx, g1, scx1, shx1, w1, wab, bab,
                                             True, _token_tile(t))
    _, _, kc, vc, _, cumfc, cumbc = _out_proj(oac, obc, w0, ctx, gtc, g1, scc1, shc1, w1, wab, bab,
                                              False, _token_tile(ctx_len))
    o = _gla(q, k, v, cumf, cumb, gate, kc, vc, cumfc, cumbc, gla_out_norm[0][None, :].astype(F32))
    return _odd_out(o, w_out[1].astype(BF16), x, gtx1, _token_tile(t))
```

```python
import functools
import math

import jax
import jax.numpy as jnp
from jax import lax
from jax.experimental import pallas as pl
from jax.experimental.pallas import tpu as pltpu

D_MODEL = 1024
GRID_W = 64
HEAD_DIM = 64
ROPE_BASE = 10000.0
EPS = 1e-6
NEG_INF = -1e30
ATTN_SCALE = HEAD_DIM ** -0.5
LOG2E = math.log2(math.e)
WINDOW = 128
Q_BLOCK = 128

A_HEADS = 8
A_KV_HEADS = 2
A_GROUP = A_HEADS // A_KV_HEADS
A_WIDTH = A_HEADS * HEAD_DIM
A_KV_WIDTH = A_KV_HEADS * HEAD_DIM
B_HEADS = 4
B_VDIM = 2 * HEAD_DIM
B_WIDTH = B_HEADS * B_VDIM
EVEN_IN = 2 * A_WIDTH + 2 * A_KV_WIDTH + 4 * B_WIDTH

G_HEADS = 4
G_DK = 128
G_DV = 256
G_RANK = 16
G_TAU = 16.0
G_CHUNK = 64
G_BLOCK = 256
G_QK = G_HEADS * G_DK
G_V = G_HEADS * G_DV
ODD_MAIN = 2 * G_QK + 2 * G_V
ODD_PAD = ODD_MAIN + 128

VMEM_LIMIT_BYTES = 56 * 1024 * 1024

BF16 = jnp.bfloat16
F32 = jnp.float32


def _params(*semantics):
    return pltpu.CompilerParams(dimension_semantics=semantics,
                                vmem_limit_bytes=VMEM_LIMIT_BYTES)


def _silu(v):
    return v * (1.0 / (1.0 + jnp.exp(-v)))


def _dot(a, b):
    return jnp.dot(a, b, preferred_element_type=F32)


def _dot_nt(a, b):
    return lax.dot_general(a, b, (((1,), (1,)), ((), ())), preferred_element_type=F32)


def _dot_tn(a, b):
    return lax.dot_general(a, b, (((0,), (0,)), ((), ())), preferred_element_type=F32)


def _split_bf16(v):
    hi = v.astype(BF16)
    lo = (v - hi.astype(F32)).astype(BF16)
    return hi, lo


def _adaln_kernel(cond_ref, w_ref, b_ref, o_ref):
    a = _silu(cond_ref[...])
    a_hi, a_lo = _split_bf16(a)
    w_hi, w_lo = _split_bf16(w_ref[...])
    o_ref[...] = (_dot(a_hi, w_hi) + _dot(a_hi, w_lo) + _dot(a_lo, w_hi)) + b_ref[...]


def _adaln(cond, w, b):
    n_layers = w.shape[0]
    rows = cond.shape[0]
    tn = 512
    return pl.pallas_call(
        _adaln_kernel,
        grid=(n_layers, 3 * D_MODEL // tn),
        in_specs=[
            pl.BlockSpec((rows, D_MODEL), lambda l, n: (0, 0)),
            pl.BlockSpec((None, D_MODEL, tn), lambda l, n: (l, 0, n)),
            pl.BlockSpec((None, 1, tn), lambda l, n: (l, 0, n)),
        ],
        out_specs=pl.BlockSpec((None, rows, tn), lambda l, n: (l, 0, n)),
        out_shape=jax.ShapeDtypeStruct((n_layers, rows, 3 * D_MODEL), F32),
        compiler_params=_params("arbitrary", "arbitrary"),
        name="adaln",
    )(cond, w, b)


def _modulate(xf, g_ref, sc_ref, sh_ref):
    r = lax.rsqrt(jnp.mean(xf * xf, axis=-1, keepdims=True) + EPS)
    return ((xf * r) * g_ref[...] * (1.0 + sc_ref[...]) + sh_ref[...]).astype(BF16)


def _modulated(x_ref, g_ref, sc_ref, sh_ref):
    return _modulate(x_ref[...], g_ref, sc_ref, sh_ref)


def _interleave(*gens):
    results = [None] * len(gens)
    live = list(range(len(gens)))
    while live:
        for idx in list(live):
            try:
                next(gens[idx])
            except StopIteration as stop:
                results[idx] = stop.value
                live.remove(idx)
    return results


def _mod_spec(per_batch):
    if per_batch:
        return pl.BlockSpec((None, 1, D_MODEL), lambda j, b: (b, 0, 0))
    return pl.BlockSpec((None, 1, D_MODEL), lambda j, b: (0, 0, 0))


def _rope_head(blk, c, s):
    r = lax.rsqrt(jnp.mean(blk * blk, axis=0, keepdims=True) + EPS)
    partner = jnp.concatenate([blk[16:32], blk[0:16], blk[48:64], blk[32:48]], axis=0)
    return (blk * c + partner * s) * r


def _even_proj_kernel(x_ref, g_ref, sc_ref, sh_ref, wt_ref, tab_ref,
                      qa_ref, ka_ref, va_ref, ga_ref, qb_ref, kb_ref, vb_ref, gb_ref):
    h = _modulated(x_ref, g_ref, sc_ref, sh_ref)

    def rows(start, size):
        return _dot_nt(wt_ref[start:start + size, :], h)

    def normed(acc, table):
        c = tab_ref[2 * table]
        s = tab_ref[2 * table + 1]
        n = acc.shape[0] // HEAD_DIM
        return jnp.concatenate(
            [_rope_head(acc[HEAD_DIM * i:HEAD_DIM * (i + 1)], c, s) for i in range(n)], axis=0)

    off = 0
    qa_ref[...] = normed(rows(off, A_WIDTH), 0).astype(BF16)
    off += A_WIDTH
    kv = rows(off, 2 * A_KV_WIDTH)
    ka_ref[...] = normed(kv[0:A_KV_WIDTH], 1).T.astype(BF16)
    va_ref[...] = kv[A_KV_WIDTH:].astype(BF16)
    off += 2 * A_KV_WIDTH
    ga_ref[...] = rows(off, A_WIDTH).astype(BF16)
    off += A_WIDTH
    qb_ref[...] = normed(rows(off, B_WIDTH), 2).astype(BF16)
    off += B_WIDTH
    kb_ref[...] = normed(rows(off, B_WIDTH), 3).T.astype(BF16)
    off += B_WIDTH
    vb_ref[...] = rows(off, B_WIDTH).astype(BF16)
    off += B_WIDTH
    gb_ref[...] = rows(off, B_WIDTH).astype(BF16)


def _even_proj(x, norm_g, sc, sh, wt, tables, per_batch, tm):
    bn, t, _ = x.shape
    nt = t // tm

    def fm(width):
        return (pl.BlockSpec((None, width, tm), lambda j, b: (b, 0, j)),
                jax.ShapeDtypeStruct((bn, width, t), BF16))

    def tk(width):
        return (pl.BlockSpec((None, tm, width), lambda j, b: (b, j, 0)),
                jax.ShapeDtypeStruct((bn, t, width), BF16))

    outs = [fm(A_WIDTH), tk(A_KV_WIDTH), fm(A_KV_WIDTH), fm(A_WIDTH),
            fm(B_WIDTH), tk(B_WIDTH), fm(B_WIDTH), fm(B_WIDTH)]
    return pl.pallas_call(
        _even_proj_kernel,
        grid=(nt, bn),
        in_specs=[
            pl.BlockSpec((None, tm, D_MODEL), lambda j, b: (b, j, 0)),
            pl.BlockSpec((1, D_MODEL), lambda j, b: (0, 0)),
            _mod_spec(per_batch), _mod_spec(per_batch),
            pl.BlockSpec((EVEN_IN, D_MODEL), lambda j, b: (0, 0)),
            pl.BlockSpec((8, HEAD_DIM, tm), lambda j, b: (0, 0, j)),
        ],
        out_specs=[o[0] for o in outs],
        out_shape=[o[1] for o in outs],
        compiler_params=_params("arbitrary", "arbitrary"),
        name="even_proj",
    )(x, norm_g, sc, sh, wt, tables)


def _rope_tables(gains, t, rotary, q_scale):
    m = HEAD_DIM // 4
    inv = ROPE_BASE ** (-jnp.arange(m, dtype=F32) / m)
    pos = jnp.arange(t, dtype=jnp.int32)
    rows = (pos // GRID_W).astype(F32)
    cols = (pos % GRID_W).astype(F32)
    if rotary:
        ang_r = inv[:, None] * rows[None, :]
        ang_c = inv[:, None] * cols[None, :]
        cos = jnp.concatenate([jnp.cos(ang_r), jnp.cos(ang_r), jnp.cos(ang_c), jnp.cos(ang_c)], axis=0)
        sin = jnp.concatenate([-jnp.sin(ang_r), jnp.sin(ang_r), -jnp.sin(ang_c), jnp.sin(ang_c)], axis=0)
    else:
        cos = jnp.ones((HEAD_DIM, t), F32)
        sin = jnp.zeros((HEAD_DIM, t), F32)
    out = []
    for g, scale in zip(gains, q_scale):
        g = g.astype(F32)
        gp = jnp.concatenate([g[16:32], g[0:16], g[48:64], g[32:48]])
        out.append(g[:, None] * cos * scale)
        out.append(gp[:, None] * sin * scale)
    return jnp.stack(out)


def _win_attn_kernel(*refs, tq, windowed):
    width = A_HEADS * tq
    if windowed:
        (q_ref, qn_ref, kp_ref, kc_ref, kn_ref, kx_ref, vp_ref, vc_ref, vn_ref, vx_ref,
         g_ref, sink_ref, o_ref, qpad_a, qpad_b, s_a, s_b, m_a, m_b) = refs
    else:
        (q_ref, kx_ref, vx_ref, g_ref, sink_ref, o_ref, qpad_a, qpad_b, s_a, s_b, m_a, m_b) = refs
    buf_a = (qpad_a, s_a, m_a)
    buf_b = (qpad_b, s_b, m_b)

    def scores(load_q, key_segs, buf):
        qpad_ref, s_ref, m_ref = buf
        q = load_q()
        qpad_ref[...] = jnp.zeros_like(qpad_ref)
        for hd in range(A_HEADS):
            grp = hd // A_GROUP
            qpad_ref[HEAD_DIM * grp:HEAD_DIM * (grp + 1), tq * hd:tq * (hd + 1)] = (
                q[HEAD_DIM * hd:HEAD_DIM * (hd + 1), :])
        qpad = qpad_ref[...]
        m = sink_ref[...]
        row = 0
        for load_key, kind, valid in key_segs:
            s = _dot(load_key(), qpad)
            n = s.shape[0]
            if kind is not None:
                r = lax.broadcasted_iota(jnp.int32, s.shape, 0)
                cq = lax.broadcasted_iota(jnp.int32, s.shape, 1) % tq
                off = 0 if valid is None else jnp.where(valid, 0, tq)
                ok = (r >= cq + off) if kind == "prev" else (r <= cq - off)
                s = jnp.where(ok, s, NEG_INF)
            s_ref[row:row + n, :] = s
            m = jnp.maximum(m, jnp.max(s, axis=0, keepdims=True))
            row += n
            yield
        m_ref[...] = m

    def attend(value_segs, buf, col0):
        _, s_ref, m_ref = buf
        m = m_ref[...]
        l = jnp.exp2(sink_ref[...] - m)
        acc = jnp.zeros((A_KV_WIDTH, width), F32)
        row = 0
        for load_value in value_segs:
            v = load_value()
            n = v.shape[1]
            p = jnp.exp2(s_ref[row:row + n, :] - m)
            l = l + jnp.sum(p, axis=0, keepdims=True)
            p = p.astype(BF16)
            yield
            acc = acc + _dot(v, p)
            row += n
        inv = 1.0 / l
        for hd in range(A_HEADS):
            grp = hd // A_GROUP
            o = acc[HEAD_DIM * grp:HEAD_DIM * (grp + 1), tq * hd:tq * (hd + 1)]
            o = o * inv[:, tq * hd:tq * (hd + 1)]
            gate = g_ref[HEAD_DIM * hd:HEAD_DIM * (hd + 1), col0:col0 + tq].astype(F32)
            o_ref[HEAD_DIM * hd:HEAD_DIM * (hd + 1), col0:col0 + tq] = (o * _silu(gate)).astype(BF16)

    ctx_keys = (lambda: kx_ref[...], None, None)
    ctx_vals = lambda: vx_ref[...]
    if not windowed:
        _interleave(scores(lambda: q_ref[...], [ctx_keys], buf_a))
        _interleave(attend([ctx_vals], buf_a, 0))
        return

    j = pl.program_id(1)
    last = pl.num_programs(1) - 1
    cur_lo = lambda: kc_ref[0:tq, :]
    cur_hi = lambda: kc_ref[tq:, :]
    nxt_lo = lambda: kn_ref[0:tq, :]
    nxt_hi = lambda: kn_ref[tq:, :]
    x_keys = [(lambda: kp_ref[...], "prev", j > 0), (cur_lo, None, None), (cur_hi, "next", None), ctx_keys]
    y_keys = [(cur_lo, "prev", None), (cur_hi, None, None), (nxt_lo, "next", j < last), ctx_keys]
    z_keys = [(cur_hi, "prev", None), (nxt_lo, None, None), (nxt_hi, "next", None), ctx_keys]
    x_vals = [lambda: vp_ref[...], lambda: vc_ref[:, 0:tq], lambda: vc_ref[:, tq:], ctx_vals]
    y_vals = [lambda: vc_ref[:, 0:tq], lambda: vc_ref[:, tq:], lambda: vn_ref[:, 0:tq], ctx_vals]

    @pl.when(j == 0)
    def _():
        _interleave(scores(lambda: q_ref[:, 0:tq], x_keys, buf_a))

    _interleave(scores(lambda: q_ref[:, tq:], y_keys, buf_b), attend(x_vals, buf_a, 0))
    _interleave(scores(lambda: qn_ref[...], z_keys, buf_a), attend(y_vals, buf_b, tq))


def _win_attn(qa, ka, va, kac, vac, ga, sink_row, windowed):
    bn, _, t = qa.shape
    ctx_len = kac.shape[1]
    kx_spec = pl.BlockSpec((None, ctx_len, A_KV_WIDTH), lambda b, j: (b, 0, 0))
    vx_spec = pl.BlockSpec((None, A_KV_WIDTH, ctx_len), lambda b, j: (b, 0, 0))
    if windowed:
        tq = Q_BLOCK
        nb = t // tq
        assert nb % 2 == 0
        steps = nb // 2
        n_keys = 3 * tq + ctx_len
        before = lambda j: jnp.maximum(2 * j - 1, 0)
        after = lambda j: jnp.minimum(j + 1, steps - 1)
        pair_spec = pl.BlockSpec((None, A_WIDTH, 2 * tq), lambda b, j: (b, 0, j))
        in_specs = [pair_spec,
                    pl.BlockSpec((None, A_WIDTH, tq), lambda b, j: (b, 0, jnp.minimum(2 * j + 2, nb - 1))),
                    pl.BlockSpec((None, tq, A_KV_WIDTH), lambda b, j: (b, before(j), 0)),
                    pl.BlockSpec((None, 2 * tq, A_KV_WIDTH), lambda b, j: (b, j, 0)),
                    pl.BlockSpec((None, 2 * tq, A_KV_WIDTH), lambda b, j: (b, after(j), 0)),
                    kx_spec,
                    pl.BlockSpec((None, A_KV_WIDTH, tq), lambda b, j: (b, 0, before(j))),
                    pl.BlockSpec((None, A_KV_WIDTH, 2 * tq), lambda b, j: (b, 0, j)),
                    pl.BlockSpec((None, A_KV_WIDTH, 2 * tq), lambda b, j: (b, 0, after(j))),
                    vx_spec, pair_spec]
        args = [qa, qa, ka, ka, ka, kac, va, va, va, vac, ga, sink_row]
        out_spec = pair_spec
    else:
        tq = t
        steps = 1
        n_keys = ctx_len
        out_spec = pl.BlockSpec((None, A_WIDTH, tq), lambda b, j: (b, 0, 0))
        in_specs = [out_spec, kx_spec, vx_spec, out_spec]
        args = [qa, kac, vac, ga, sink_row]
    width = A_HEADS * tq
    return pl.pallas_call(
        functools.partial(_win_attn_kernel, tq=tq, windowed=windowed),
        grid=(bn, steps),
        in_specs=in_specs + [pl.BlockSpec((1, width), lambda b, j: (0, 0))],
        out_specs=out_spec,
        out_shape=jax.ShapeDtypeStruct((bn, A_WIDTH, t), BF16),
        scratch_shapes=[pltpu.VMEM((A_KV_WIDTH, width), BF16), pltpu.VMEM((A_KV_WIDTH, width), BF16),
                        pltpu.VMEM((n_keys, width), F32), pltpu.VMEM((n_keys, width), F32),
                        pltpu.VMEM((1, width), F32), pltpu.VMEM((1, width), F32)],
        compiler_params=_params("arbitrary", "arbitrary"),
        name="win_attn" if windowed else "ctx_sink_attn",
    )(*args)


def _key_chunks(segments, key_chunk):
    chunks = []
    base = 0
    for n, load_k, load_v in segments:
        for c0 in range(0, n, key_chunk):
            size = min(key_chunk, n - c0)
            chunks.append((functools.partial(load_k, c0, size), functools.partial(load_v, c0, size),
                           base + c0, size))
        base += n
    return chunks


def _diff_scores(q, chunks, buf, tq):
    qpad_ref, s_ref, m_ref = buf
    qpad_ref[...] = jnp.zeros_like(qpad_ref)
    qpad_ref[0:HEAD_DIM, 0:tq] = q[0:HEAD_DIM, :]
    qpad_ref[HEAD_DIM:, tq:] = q[HEAD_DIM:, :]
    qpad = qpad_ref[...]
    m = None
    for load_k, _, row, size in chunks:
        s = _dot(load_k(), qpad)
        s_ref[row:row + size, :] = s
        s_max = jnp.max(s, axis=0, keepdims=True)
        m = s_max if m is None else jnp.maximum(m, s_max)
        yield
    m_ref[...] = m


def _diff_attend(chunks, buf, tq, lam, emit):
    _, s_ref, m_ref = buf
    m = m_ref[...]
    l = jnp.zeros((1, 2 * tq), F32)
    acc = jnp.zeros((B_VDIM, 2 * tq), F32)
    for _, load_v, row, size in chunks:
        p = jnp.exp2(s_ref[row:row + size, :] - m)
        l = l + jnp.sum(p, axis=0, keepdims=True)
        p = p.astype(BF16)
        yield
        acc = acc + _dot(load_v(), p)
    acc = acc * (1.0 / l)
    emit(acc[:, 0:tq] - lam * acc[:, tq:])


def _diff_emit(g_ref, w_ref, o_ref, rows, cols):
    def emit(o):
        r = lax.rsqrt(jnp.mean(o * o, axis=0, keepdims=True) + EPS)
        o = (o * r) * w_ref[...]
        o_ref[rows, cols] = (o * _silu(g_ref[rows, cols].astype(F32))).astype(BF16)
    return emit


def _diff_attn_kernel(q_ref, qn_ref, k_ref, kx_ref, kz_ref, kxz_ref, v_ref, vx_ref,
                      g_ref, w_ref, lam_ref, o_ref, qpad_a, qpad_b, s_a, s_b, m_a, m_b,
                      *, tq, key_chunk):
    buf_a = (qpad_a, s_a, m_a)
    buf_b = (qpad_b, s_b, m_b)

    def chunks_of(keys, ctx_keys):
        return _key_chunks(
            [(keys.shape[0], lambda c0, n: keys[c0:c0 + n, :], lambda c0, n: v_ref[:, c0:c0 + n]),
             (ctx_keys.shape[0], lambda c0, n: ctx_keys[c0:c0 + n, :], lambda c0, n: vx_ref[:, c0:c0 + n])],
            key_chunk)

    cur = chunks_of(k_ref, kx_ref)
    ahead = chunks_of(kz_ref, kxz_ref)
    lam = lam_ref[...]
    rows = slice(0, B_VDIM)

    @pl.when((pl.program_id(0) == 0) & (pl.program_id(1) == 0) & (pl.program_id(2) == 0))
    def _():
        _interleave(_diff_scores(q_ref[:, 0:tq], cur, buf_a, tq))

    _interleave(_diff_scores(q_ref[:, tq:], cur, buf_b, tq),
                _diff_attend(cur, buf_a, tq, lam, _diff_emit(g_ref, w_ref, o_ref, rows, slice(0, tq))))
    _interleave(_diff_scores(qn_ref[...], ahead, buf_a, tq),
                _diff_attend(cur, buf_b, tq, lam, _diff_emit(g_ref, w_ref, o_ref, rows, slice(tq, 2 * tq))))


def _diff_attn(qb, kb, vb, kbc, vbc, gb, w_sub, lam_row, tq):
    bn, _, t = qb.shape
    t_keys = kb.shape[1]
    ctx_len = kbc.shape[1]
    assert t % (2 * tq) == 0
    nq = t // (2 * tq)

    def following(b, h):
        b_next = jnp.minimum(b + (h + 1) // B_HEADS, bn - 1)
        return b_next, jnp.where(b + (h + 1) // B_HEADS > bn - 1, h, (h + 1) % B_HEADS)

    def next_tile(b, h, j):
        b_next, h_next = following(b, h)
        wrap = j == nq - 1
        return jnp.where(wrap, b_next, b), jnp.where(wrap, h_next, h), jnp.where(wrap, 0, 2 * (j + 1))

    def keys_ahead(b, h, j):
        b_next, h_next, _ = next_tile(b, h, j)
        return b_next, 0, h_next

    head_fm = lambda width: pl.BlockSpec((None, B_VDIM, width), lambda b, h, j: (b, h, 0))
    pair_spec = pl.BlockSpec((None, B_VDIM, 2 * tq), lambda b, h, j: (b, h, j))
    in_specs = [
        pair_spec,
        pl.BlockSpec((None, B_VDIM, tq), lambda b, h, j: next_tile(b, h, j)),
        pl.BlockSpec((None, t_keys, B_VDIM), lambda b, h, j: (b, 0, h)),
        pl.BlockSpec((None, ctx_len, B_VDIM), lambda b, h, j: (b, 0, h)),
        pl.BlockSpec((None, t_keys, B_VDIM), lambda b, h, j: keys_ahead(b, h, j)),
        pl.BlockSpec((None, ctx_len, B_VDIM), lambda b, h, j: keys_ahead(b, h, j)),
        head_fm(t_keys), head_fm(ctx_len), pair_spec,
        pl.BlockSpec((B_VDIM, tq), lambda b, h, j: (0, 0)),
        pl.BlockSpec((1, tq), lambda b, h, j: (0, 0)),
    ]
    n_keys = t_keys + ctx_len
    return pl.pallas_call(
        functools.partial(_diff_attn_kernel, tq=tq, key_chunk=512),
        grid=(bn, B_HEADS, nq),
        in_specs=in_specs,
        out_specs=pair_spec,
        out_shape=jax.ShapeDtypeStruct((bn, B_WIDTH, t), BF16),
        scratch_shapes=[pltpu.VMEM((B_VDIM, 2 * tq), BF16), pltpu.VMEM((B_VDIM, 2 * tq), BF16),
                        pltpu.VMEM((n_keys, 2 * tq), F32), pltpu.VMEM((n_keys, 2 * tq), F32),
                        pltpu.VMEM((1, 2 * tq), F32), pltpu.VMEM((1, 2 * tq), F32)],
        compiler_params=_params("arbitrary", "arbitrary", "arbitrary"),
        name="diff_attn",
    )(qb, qb, kb, kbc, kb, kbc, vb, vbc, gb, w_sub, lam_row)


def _ctx_diff_attn_kernel(q_ref, kx_ref, vx_ref, g_ref, w_ref, lam_ref, o_ref,
                          qpad_a, qpad_b, s_a, s_b, m_a, m_b, *, tq, key_chunk):
    bufs = [(qpad_a, s_a, m_a), (qpad_b, s_b, m_b)]
    lam = lam_ref[...]
    n = kx_ref.shape[0]

    def head(h):
        rows = slice(B_VDIM * h, B_VDIM * (h + 1))
        chunks = _key_chunks([(n, lambda c0, size: kx_ref[c0:c0 + size, rows],
                               lambda c0, size: vx_ref[rows, c0:c0 + size])], key_chunk)
        return rows, chunks

    rows, chunks = head(0)
    _interleave(_diff_scores(q_ref[rows, :], chunks, bufs[0], tq))
    for h in range(B_HEADS):
        rows, chunks = head(h)
        gens = [_diff_attend(chunks, bufs[h % 2], tq, lam,
                             _diff_emit(g_ref, w_ref, o_ref, rows, slice(0, tq)))]
        if h + 1 < B_HEADS:
            rows_n, chunks_n = head(h + 1)
            gens.insert(0, _diff_scores(q_ref[rows_n, :], chunks_n, bufs[(h + 1) % 2], tq))
        _interleave(*gens)


def _ctx_diff_attn(qbc, kbc, vbc, gbc, w_sub, lam_row):
    bn, _, tq = qbc.shape
    n_keys = kbc.shape[1]
    fm = pl.BlockSpec((None, B_WIDTH, tq), lambda b: (b, 0, 0))
    return pl.pallas_call(
        functools.partial(_ctx_diff_attn_kernel, tq=tq, key_chunk=512),
        grid=(bn,),
        in_specs=[fm, pl.BlockSpec((None, n_keys, B_WIDTH), lambda b: (b, 0, 0)),
                  pl.BlockSpec((None, B_WIDTH, n_keys), lambda b: (b, 0, 0)), fm,
                  pl.BlockSpec((B_VDIM, tq), lambda b: (0, 0)),
                  pl.BlockSpec((1, tq), lambda b: (0, 0))],
        out_specs=fm,
        out_shape=jax.ShapeDtypeStruct((bn, B_WIDTH, tq), BF16),
        scratch_shapes=[pltpu.VMEM((B_VDIM, 2 * tq), BF16), pltpu.VMEM((B_VDIM, 2 * tq), BF16),
                        pltpu.VMEM((n_keys, 2 * tq), F32), pltpu.VMEM((n_keys, 2 * tq), F32),
                        pltpu.VMEM((1, 2 * tq), F32), pltpu.VMEM((1, 2 * tq), F32)],
        compiler_params=_params("arbitrary"),
        name="ctx_diff_attn",
    )(qbc, kbc, vbc, gbc, w_sub, lam_row)


def _out_proj_kernel(oa_ref, ob_ref, w0_ref, x_ref, gt_ref, g_ref, sc_ref, sh_ref,
                     w_ref, wab_ref, bab_ref,
                     xo_ref, q_ref, k_ref, v_ref, gate_ref, cumf_ref, cumb_ref):
    y = _dot_tn(oa_ref[...], w0_ref[0:A_WIDTH, :]) + _dot_tn(ob_ref[...], w0_ref[A_WIDTH:, :])
    x_new = x_ref[...] + gt_ref[...] * y
    xo_ref[...] = x_new
    h = _modulate(x_new, g_ref, sc_ref, sh_ref)

    def cols(start, size):
        return _dot(h, w_ref[:, start:start + size])

    q_ref[...] = (cols(0, G_QK) * (G_DK ** -0.5)).astype(BF16)
    k_ref[...] = cols(G_QK, G_QK).astype(BF16)
    for i in range(G_V // 512):
        v_ref[:, 512 * i:512 * (i + 1)] = cols(2 * G_QK + 512 * i, 512).astype(BF16)
    for i in range(G_V // 512):
        gate_ref[:, 512 * i:512 * (i + 1)] = cols(2 * G_QK + G_V + 512 * i, 512).astype(BF16)
    low = cols(ODD_MAIN, 128).astype(BF16)
    z = _dot(low, wab_ref[...]) + bab_ref[...]
    la = (jnp.minimum(z, 0.0) - jnp.log(1.0 + jnp.exp(-jnp.abs(z)))) / G_TAU
    r = lax.broadcasted_iota(jnp.int32, (G_BLOCK, G_BLOCK), 0)
    c = lax.broadcasted_iota(jnp.int32, (G_BLOCK, G_BLOCK), 1)
    in_chunk = r % G_CHUNK
    tri_f = jnp.where((r - c).astype(jnp.uint32) <= in_chunk.astype(jnp.uint32), 1.0, 0.0)
    tri_b = jnp.where((c - r).astype(jnp.uint32) <= (G_CHUNK - 1 - in_chunk).astype(jnp.uint32),
                      1.0, 0.0)
    for blk in range(la.shape[0] // G_BLOCK):
        rows = slice(blk * G_BLOCK, (blk + 1) * G_BLOCK)
        for tri, lo_col, out_ref in ((tri_f, 0, cumf_ref), (tri_b, G_QK, cumb_ref)):
            hi, lo = _split_bf16(la[rows, lo_col:lo_col + G_QK])
            cum = _dot(tri.astype(BF16), jnp.concatenate([hi, lo], axis=1))
            out_ref[rows, :] = cum[:, :G_QK] + cum[:, G_QK:]


def _out_proj(oa, ob, w0, x, gate, norm_g, sc, sh, w, wab, bab, per_batch, tm):
    bn, t, _ = x.shape
    fm = pl.BlockSpec((None, A_WIDTH, tm), lambda j, b: (b, 0, j))

    def tk(width, dtype):
        return (pl.BlockSpec((None, tm, width), lambda j, b: (b, j, 0)),
                jax.ShapeDtypeStruct((bn, t, width), dtype))

    const = lambda rows, cols: pl.BlockSpec((rows, cols), lambda j, b: (0, 0))
    outs = [tk(D_MODEL, F32), tk(G_QK, BF16), tk(G_QK, BF16), tk(G_V, BF16), tk(G_V, BF16),
            tk(G_QK, F32), tk(G_QK, F32)]
    return pl.pallas_call(
        _out_proj_kernel,
        grid=(t // tm, bn),
        in_specs=[
            fm, fm, const(D_MODEL, D_MODEL), outs[0][0], _mod_spec(per_batch),
            const(1, D_MODEL), _mod_spec(per_batch), _mod_spec(per_batch),
            const(D_MODEL, ODD_PAD), const(128, 2 * G_QK), const(1, 2 * G_QK),
        ],
        out_specs=[o[0] for o in outs],
        out_shape=[o[1] for o in outs],
        compiler_params=_params("arbitrary", "arbitrary"),
        name="out_proj",
    )(oa, ob, w0, x, gate, norm_g, sc, sh, w, wab, bab)


def _gla_block(q, k, v, b, state, reverse, with_out):
    n = G_BLOCK // G_CHUNK
    ends = [(j * G_CHUNK if reverse else (j + 1) * G_CHUNK - 1) for j in range(n)]
    b_last = jnp.concatenate(
        [jnp.broadcast_to(b[e:e + 1, :], (G_CHUNK, G_DK)) for e in ends], axis=0)
    kf = k.astype(F32)
    k_state = (kf * jnp.exp(b_last - b)).astype(BF16)
    if with_out:
        qd = (q.astype(F32) * jnp.exp(b)).astype(BF16)
        kd = (kf * jnp.exp(-b)).astype(BF16)
    yield
    if with_out:
        att = _dot_nt(qd, kd)
        yield
    chunks = [slice(j * G_CHUNK, (j + 1) * G_CHUNK) for j in range(n)]
    d_state = [_dot_tn(k_state[sl], v[sl]) for sl in chunks]
    decay_t = jnp.exp(b_last).T
    yield
    if with_out:
        r = lax.broadcasted_iota(jnp.int32, (G_BLOCK, G_BLOCK), 0)
        c = lax.broadcasted_iota(jnp.int32, (G_BLOCK, G_BLOCK), 1)
        in_chunk = r % G_CHUNK
        if reverse:
            causal = (c - r).astype(jnp.uint32) <= (G_CHUNK - 1 - in_chunk).astype(jnp.uint32)
        else:
            causal = (r - c).astype(jnp.uint32) <= in_chunk.astype(jnp.uint32)
        intra = _dot(jnp.where(causal, att, 0.0).astype(BF16), v)
        yield
    outs = [None] * n
    for j in (reversed(range(n)) if reverse else range(n)):
        if with_out:
            outs[j] = intra[chunks[j]] + _dot(qd[chunks[j]], state.astype(BF16))
        decay = jnp.broadcast_to(decay_t[:, ends[j]:ends[j] + 1], (G_DK, G_DV))
        state = state * decay + d_state[j]
        yield
    return (jnp.concatenate(outs, axis=0) if with_out else None), state


def _gla_kernel(q_ref, k_ref, v_ref, cumf_ref, cumb_ref, g_ref,
                kc_ref, vc_ref, cumfc_ref, cumbc_ref, w_ref, o_ref, of_ref, ob_ref):
    n_lat = q_ref.shape[0] // G_BLOCK
    n_ctx = kc_ref.shape[0] // G_BLOCK

    def rows(i):
        return pl.ds(pl.multiple_of(i * G_BLOCK, G_BLOCK), G_BLOCK)

    st_f = st_b = jnp.zeros((G_DK, G_DV), F32)
    for i in range(n_ctx):
        sl_f = slice(i * G_BLOCK, (i + 1) * G_BLOCK)
        sl_b = slice((n_ctx - 1 - i) * G_BLOCK, (n_ctx - i) * G_BLOCK)
        (_, st_f), (_, st_b) = _interleave(
            _gla_block(None, kc_ref[sl_f, :], vc_ref[sl_f, :], cumfc_ref[sl_f, :], st_f, False, False),
            _gla_block(None, kc_ref[sl_b, :], vc_ref[sl_b, :], cumbc_ref[sl_b, :], st_b, True, False))

    def finish(i, o, other_ref):
        o = o + other_ref[rows(i), :]
        rn = lax.rsqrt(jnp.mean(o * o, axis=-1, keepdims=True) + EPS)
        o = (o * rn) * w_ref[...]
        o_ref[rows(i), :] = (o * _silu(g_ref[rows(i), :].astype(F32))).astype(BF16)

    def body(i, states, second_half):
        i_b = n_lat - 1 - i
        (o_f, st_f), (o_b, st_b) = _interleave(
            _gla_block(q_ref[rows(i), :], k_ref[rows(i), :], v_ref[rows(i), :],
                       cumf_ref[rows(i), :], states[0], False, True),
            _gla_block(q_ref[rows(i_b), :], k_ref[rows(i_b), :], v_ref[rows(i_b), :],
                       cumb_ref[rows(i_b), :], states[1], True, True))
        if second_half:
            finish(i, o_f, ob_ref)
            finish(i_b, o_b, of_ref)
        else:
            of_ref[rows(i), :] = o_f
            ob_ref[rows(i_b), :] = o_b
        return st_f, st_b

    assert n_lat % 2 == 0
    states = lax.fori_loop(0, n_lat // 2, functools.partial(body, second_half=False),
                           (st_f, st_b), unroll=2)
    lax.fori_loop(n_lat // 2, n_lat, functools.partial(body, second_half=True), states, unroll=2)


def _gla(q, k, v, laf, lab, gate, kc, vc, lafc, labc, w_norm):
    bn, t, _ = q.shape
    ctx_len = kc.shape[1]
    lat = lambda width: pl.BlockSpec((None, t, width), lambda b, h: (b, 0, h))
    ctx = lambda width: pl.BlockSpec((None, ctx_len, width), lambda b, h: (b, 0, h))
    return pl.pallas_call(
        _gla_kernel,
        grid=(bn, G_HEADS),
        in_specs=[lat(G_DK), lat(G_DK), lat(G_DV), lat(G_DK), lat(G_DK), lat(G_DV),
                  ctx(G_DK), ctx(G_DV), ctx(G_DK), ctx(G_DK),
                  pl.BlockSpec((1, G_DV), lambda b, h: (0, 0))],
        out_specs=lat(G_DV),
        out_shape=jax.ShapeDtypeStruct((bn, t, G_V), BF16),
        scratch_shapes=[pltpu.VMEM((t, G_DV), F32), pltpu.VMEM((t, G_DV), F32)],
        compiler_params=_params("arbitrary", "arbitrary"),
        name="gla",
    )(q, k, v, laf, lab, gate, kc, vc, lafc, labc, w_norm)


def _odd_out_kernel(o_ref, w_ref, x_ref, gt_ref, out_ref):
    out_ref[...] = x_ref[...] + gt_ref[...] * _dot(o_ref[...], w_ref[...])


def _odd_out(o, w, x, gate, tm):
    bn, t, _ = x.shape
    xs = pl.BlockSpec((None, tm, D_MODEL), lambda j, b: (b, j, 0))
    return pl.pallas_call(
        _odd_out_kernel,
        grid=(t // tm, bn),
        in_specs=[xs, pl.BlockSpec((D_MODEL, D_MODEL), lambda j, b: (0, 0)), xs, _mod_spec(True)],
        out_specs=xs,
        out_shape=jax.ShapeDtypeStruct(x.shape, F32),
        compiler_params=_params("arbitrary", "arbitrary"),
        name="odd_out",
    )(o, w, x, gate)


def _token_tile(t):
    return 512 if t % 512 == 0 else t


def kernel(x, c, ctx, c_ctx, adaln_w, adaln_b, norm_g, w_out, ab_w_in, a_q_norm, a_k_norm, a_sink,
           b_q_norm, b_k_norm, b_lambda_q1, b_lambda_k1, b_lambda_q2, b_lambda_k2, b_subln,
           gla_w_in, gla_wa_f, gla_ba_f, gla_wa_b, gla_ba_b, gla_out_norm):
    bn, t, _ = x.shape
    ctx_len = ctx.shape[1]
    depth = adaln_w.shape[0]
    assert depth == 2

    pad_rows = (-(bn + 1)) % 8
    cond = jnp.concatenate([c, c_ctx[None, :], jnp.zeros((pad_rows, D_MODEL), F32)], axis=0)
    mod = _adaln(cond, adaln_w, adaln_b[:, None, :])

    def mods(layer):
        m = mod[layer]
        shift, scale, gate = (m[:, i * D_MODEL:(i + 1) * D_MODEL] for i in range(3))
        per_x = tuple(v[:bn, None, :] for v in (shift, scale, gate))
        per_c = tuple(v[bn:bn + 1, None, :] for v in (shift, scale, gate))
        return per_x, per_c

    (shx, scx, gtx), (shc, scc, gtc) = mods(0)
    lambda_init = 0.8 - 0.6 * math.exp(-0.3 * 0)
    wt = ab_w_in[0].T.astype(BF16)
    gains = [a_q_norm[0], a_k_norm[0], b_q_norm[0], b_k_norm[0]]
    q_scale = [ATTN_SCALE * LOG2E, 1.0, ATTN_SCALE * LOG2E, 1.0]
    tab_x = _rope_tables(gains, t, True, q_scale)
    tab_c = _rope_tables(gains, ctx_len, False, q_scale)
    g0 = norm_g[0][None, :]
    qa, ka, va, ga, qb, kb, vb, gb = _even_proj(x, g0, scx, shx, wt, tab_x, True, _token_tile(t))
    qac, kac, vac, gac, qbc, kbc, vbc, gbc = _even_proj(ctx, g0, scc, shc, wt, tab_c, False,
                                                        _token_tile(ctx_len))

    sink = a_sink[0].astype(F32) * LOG2E
    oa = _win_attn(qa, ka, va, kac, vac, ga, jnp.repeat(sink, Q_BLOCK)[None, :], True)
    oac = _win_attn(qac, None, None, kac, vac, gac, jnp.repeat(sink, ctx_len)[None, :], False)

    lam = (jnp.exp(jnp.sum(b_lambda_q1[0].astype(F32) * b_lambda_k1[0].astype(F32)))
           - jnp.exp(jnp.sum(b_lambda_q2[0].astype(F32) * b_lambda_k2[0].astype(F32))) + lambda_init)
    tq = 256
    w_sub = jnp.broadcast_to((b_subln[0].astype(F32) * (1.0 - lambda_init))[:, None], (B_VDIM, tq))
    lam_row = jnp.broadcast_to(lam, (1, tq)).astype(F32)
    ob = _diff_attn(qb, kb, vb, kbc, vbc, gb, w_sub, lam_row, tq)
    obc = _ctx_diff_attn(qbc, kbc, vbc, gbc, w_sub, lam_row)

    w0 = w_out[0].astype(BF16)
    (shx1, scx1, gtx1), (shc1, scc1, _) = mods(1)
    w1 = jnp.pad(gla_w_in[0], ((0, 0), (0, ODD_PAD - gla_w_in.shape[2]))).astype(BF16)
    wab = jnp.zeros((128, 2 * G_QK), F32)
    wab = wab.at[0:G_RANK, 0:G_QK].set(gla_wa_f[0])
    wab = wab.at[G_RANK:2 * G_RANK, G_QK:].set(gla_wa_b[0]).astype(BF16)
    bab = jnp.concatenate([gla_ba_f[0], gla_ba_b[0]])[None, :].astype(F32)
    g1 = norm_g[1][None, :]
    x, q, k, v, gate, cumf, cumb = _out_proj(oa, ob, w0, x, gtx, g1, scx1, shx1, w1, wab, bab,
                                             True, _token_tile(t))
    _, _, kc, vc, _, cumfc, cumbc = _out_proj(oac, obc, w0, ctx, gtc, g1, scc1, shc1, w1, wab, bab,
                                              False, _token_tile(ctx_len))
    o = _gla(q, k, v, cumf, cumb, gate, kc, vc, cumfc, cumbc, gla_out_norm[0][None, :].astype(F32))
    return _odd_out(o, w_out[1].astype(BF16), x, gtx1, _token_tile(t))
```

```python
import functools
import math

import jax
import jax.numpy as jnp
from jax import lax
from jax.experimental import pallas as pl
from jax.experimental.pallas import tpu as pltpu

D_MODEL = 1024
GRID_W = 64
HEAD_DIM = 64
ROPE_BASE = 10000.0
EPS = 1e-6
NEG_INF = -1e30
ATTN_SCALE = HEAD_DIM ** -0.5
LOG2E = math.log2(math.e)
WINDOW = 128
Q_BLOCK = 128

A_HEADS = 8
A_KV_HEADS = 2
A_GROUP = A_HEADS // A_KV_HEADS
A_WIDTH = A_HEADS * HEAD_DIM
A_KV_WIDTH = A_KV_HEADS * HEAD_DIM
B_HEADS = 4
B_VDIM = 2 * HEAD_DIM
B_WIDTH = B_HEADS * B_VDIM
EVEN_IN = 2 * A_WIDTH + 2 * A_KV_WIDTH + 4 * B_WIDTH

G_HEADS = 4
G_DK = 128
G_DV = 256
G_RANK = 16
G_TAU = 16.0
G_CHUNK = 64
G_BLOCK = 256
G_QK = G_HEADS * G_DK
G_V = G_HEADS * G_DV
ODD_MAIN = 2 * G_QK + 2 * G_V
ODD_PAD = ODD_MAIN + 128

VMEM_LIMIT_BYTES = 56 * 1024 * 1024

BF16 = jnp.bfloat16
F32 = jnp.float32


def _params(*semantics):
    return pltpu.CompilerParams(dimension_semantics=semantics,
                                vmem_limit_bytes=VMEM_LIMIT_BYTES)


def _silu(v):
    return v * (1.0 / (1.0 + jnp.exp(-v)))


def _dot(a, b):
    return jnp.dot(a, b, preferred_element_type=F32)


def _dot_nt(a, b):
    return lax.dot_general(a, b, (((1,), (1,)), ((), ())), preferred_element_type=F32)


def _dot_tn(a, b):
    return lax.dot_general(a, b, (((0,), (0,)), ((), ())), preferred_element_type=F32)


def _split_bf16(v):
    hi = v.astype(BF16)
    lo = (v - hi.astype(F32)).astype(BF16)
    return hi, lo


def _adaln_kernel(cond_ref, w_ref, b_ref, o_ref):
    a = _silu(cond_ref[...])
    a_hi, a_lo = _split_bf16(a)
    w_hi, w_lo = _split_bf16(w_ref[...])
    o_ref[...] = (_dot(a_hi, w_hi) + _dot(a_hi, w_lo) + _dot(a_lo, w_hi)) + b_ref[...]


def _adaln(cond, w, b):
    n_layers = w.shape[0]
    rows = cond.shape[0]
    tn = 512
    return pl.pallas_call(
        _adaln_kernel,
        grid=(n_layers, 3 * D_MODEL // tn),
        in_specs=[
            pl.BlockSpec((rows, D_MODEL), lambda l, n: (0, 0)),
            pl.BlockSpec((None, D_MODEL, tn), lambda l, n: (l, 0, n)),
            pl.BlockSpec((None, 1, tn), lambda l, n: (l, 0, n)),
        ],
        out_specs=pl.BlockSpec((None, rows, tn), lambda l, n: (l, 0, n)),
        out_shape=jax.ShapeDtypeStruct((n_layers, rows, 3 * D_MODEL), F32),
        compiler_params=_params("arbitrary", "arbitrary"),
        name="adaln",
    )(cond, w, b)


def _modulate(xf, g_ref, sc_ref, sh_ref):
    r = lax.rsqrt(jnp.mean(xf * xf, axis=-1, keepdims=True) + EPS)
    return ((xf * r) * g_ref[...] * (1.0 + sc_ref[...]) + sh_ref[...]).astype(BF16)


def _modulated(x_ref, g_ref, sc_ref, sh_ref):
    return _modulate(x_ref[...], g_ref, sc_ref, sh_ref)


def _interleave(*gens):
    results = [None] * len(gens)
    live = list(range(len(gens)))
    while live:
        for idx in list(live):
            try:
                next(gens[idx])
            except StopIteration as stop:
                results[idx] = stop.value
                live.remove(idx)
    return results


def _mod_spec(per_batch):
    if per_batch:
        return pl.BlockSpec((None, 1, D_MODEL), lambda j, b: (b, 0, 0))
    return pl.BlockSpec((None, 1, D_MODEL), lambda j, b: (0, 0, 0))


def _rope_head(blk, c, s):
    r = lax.rsqrt(jnp.mean(blk * blk, axis=0, keepdims=True) + EPS)
    partner = jnp.concatenate([blk[16:32], blk[0:16], blk[48:64], blk[32:48]], axis=0)
    return (blk * c + partner * s) * r


def _even_proj_kernel(x_ref, g_ref, sc_ref, sh_ref, wt_ref, tab_ref,
                      qa_ref, ka_ref, va_ref, ga_ref, qb_ref, kb_ref, vb_ref, gb_ref):
    h = _modulated(x_ref, g_ref, sc_ref, sh_ref)

    def rows(start, size):
        return _dot_nt(wt_ref[start:start + size, :], h)

    def normed(acc, table):
        c = tab_ref[2 * table]
        s = tab_ref[2 * table + 1]
        n = acc.shape[0] // HEAD_DIM
        return jnp.concatenate(
            [_rope_head(acc[HEAD_DIM * i:HEAD_DIM * (i + 1)], c, s) for i in range(n)], axis=0)

    off = 0
    qa_ref[...] = normed(rows(off, A_WIDTH), 0).astype(BF16)
    off += A_WIDTH
    kv = rows(off, 2 * A_KV_WIDTH)
    ka_ref[...] = normed(kv[0:A_KV_WIDTH], 1).T.astype(BF16)
    va_ref[...] = kv[A_KV_WIDTH:].astype(BF16)
    off += 2 * A_KV_WIDTH
    ga_ref[...] = rows(off, A_WIDTH).astype(BF16)
    off += A_WIDTH
    qb_ref[...] = normed(rows(off, B_WIDTH), 2).astype(BF16)
    off += B_WIDTH
    kb_ref[...] = normed(rows(off, B_WIDTH), 3).T.astype(BF16)
    off += B_WIDTH
    vb_ref[...] = rows(off, B_WIDTH).astype(BF16)
    off += B_WIDTH
    gb_ref[...] = rows(off, B_WIDTH).astype(BF16)


def _even_proj(x, norm_g, sc, sh, wt, tables, per_batch, tm):
    bn, t, _ = x.shape
    nt = t // tm

    def fm(width):
        return (pl.BlockSpec((None, width, tm), lambda j, b: (b, 0, j)),
                jax.ShapeDtypeStruct((bn, width, t), BF16))

    def tk(width):
        return (pl.BlockSpec((None, tm, width), lambda j, b: (b, j, 0)),
                jax.ShapeDtypeStruct((bn, t, width), BF16))

    outs = [fm(A_WIDTH), tk(A_KV_WIDTH), fm(A_KV_WIDTH), fm(A_WIDTH),
            fm(B_WIDTH), tk(B_WIDTH), fm(B_WIDTH), fm(B_WIDTH)]
    return pl.pallas_call(
        _even_proj_kernel,
        grid=(nt, bn),
        in_specs=[
            pl.BlockSpec((None, tm, D_MODEL), lambda j, b: (b, j, 0)),
            pl.BlockSpec((1, D_MODEL), lambda j, b: (0, 0)),
            _mod_spec(per_batch), _mod_spec(per_batch),
            pl.BlockSpec((EVEN_IN, D_MODEL), lambda j, b: (0, 0)),
            pl.BlockSpec((8, HEAD_DIM, tm), lambda j, b: (0, 0, j)),
        ],
        out_specs=[o[0] for o in outs],
        out_shape=[o[1] for o in outs],
        compiler_params=_params("arbitrary", "arbitrary"),
        name="even_proj",
    )(x, norm_g, sc, sh, wt, tables)


def _rope_tables(gains, t, rotary, q_scale):
    m = HEAD_DIM // 4
    inv = ROPE_BASE ** (-jnp.arange(m, dtype=F32) / m)
    pos = jnp.arange(t, dtype=jnp.int32)
    rows = (pos // GRID_W).astype(F32)
    cols = (pos % GRID_W).astype(F32)
    if rotary:
        ang_r = inv[:, None] * rows[None, :]
        ang_c = inv[:, None] * cols[None, :]
        cos = jnp.concatenate([jnp.cos(ang_r), jnp.cos(ang_r), jnp.cos(ang_c), jnp.cos(ang_c)], axis=0)
        sin = jnp.concatenate([-jnp.sin(ang_r), jnp.sin(ang_r), -jnp.sin(ang_c), jnp.sin(ang_c)], axis=0)
    else:
        cos = jnp.ones((HEAD_DIM, t), F32)
        sin = jnp.zeros((HEAD_DIM, t), F32)
    out = []
    for g, scale in zip(gains, q_scale):
        g = g.astype(F32)
        gp = jnp.concatenate([g[16:32], g[0:16], g[48:64], g[32:48]])
        out.append(g[:, None] * cos * scale)
        out.append(gp[:, None] * sin * scale)
    return jnp.stack(out)


def _win_attn_kernel(*refs, tq, windowed):
    width = A_HEADS * tq
    if windowed:
        (q_ref, qn_ref, kp_ref, kc_ref, kn_ref, kx_ref, vp_ref, vc_ref, vn_ref, vx_ref,
         g_ref, sink_ref, o_ref, qpad_a, qpad_b, s_a, s_b, m_a, m_b) = refs
    else:
        (q_ref, kx_ref, vx_ref, g_ref, sink_ref, o_ref, qpad_a, qpad_b, s_a, s_b, m_a, m_b) = refs
    buf_a = (qpad_a, s_a, m_a)
    buf_b = (qpad_b, s_b, m_b)

    def scores(load_q, key_segs, buf):
        qpad_ref, s_ref, m_ref = buf
        q = load_q()
        qpad_ref[...] = jnp.zeros_like(qpad_ref)
        for hd in range(A_HEADS):
            grp = hd // A_GROUP
            qpad_ref[HEAD_DIM * grp:HEAD_DIM * (grp + 1), tq * hd:tq * (hd + 1)] = (
                q[HEAD_DIM * hd:HEAD_DIM * (hd + 1), :])
        qpad = qpad_ref[...]
        m = sink_ref[...]
        row = 0
        for load_key, kind, valid in key_segs:
            s = _dot(load_key(), qpad)
            n = s.shape[0]
            if kind is not None:
                r = lax.broadcasted_iota(jnp.int32, s.shape, 0)
                cq = lax.broadcasted_iota(jnp.int32, s.shape, 1) % tq
                off = 0 if valid is None else jnp.where(valid, 0, tq)
                ok = (r >= cq + off) if kind == "prev" else (r <= cq - off)
                s = jnp.where(ok, s, NEG_INF)
            s_ref[row:row + n, :] = s
            m = jnp.maximum(m, jnp.max(s, axis=0, keepdims=True))
            row += n
            yield
        m_ref[...] = m

    def attend(value_segs, buf, col0):
        _, s_ref, m_ref = buf
        m = m_ref[...]
        l = jnp.exp2(sink_ref[...] - m)
        acc = jnp.zeros((A_KV_WIDTH, width), F32)
        row = 0
        for load_value in value_segs:
            v = load_value()
            n = v.shape[1]
            p = jnp.exp2(s_ref[row:row + n, :] - m)
            l = l + jnp.sum(p, axis=0, keepdims=True)
            p = p.astype(BF16)
            yield
            acc = acc + _dot(v, p)
            row += n
        inv = 1.0 / l
        for hd in range(A_HEADS):
            grp = hd // A_GROUP
            o = acc[HEAD_DIM * grp:HEAD_DIM * (grp + 1), tq * hd:tq * (hd + 1)]
            o = o * inv[:, tq * hd:tq * (hd + 1)]
            gate = g_ref[HEAD_DIM * hd:HEAD_DIM * (hd + 1), col0:col0 + tq].astype(F32)
            o_ref[HEAD_DIM * hd:HEAD_DIM * (hd + 1), col0:col0 + tq] = (o * _silu(gate)).astype(BF16)

    ctx_keys = (lambda: kx_ref[...], None, None)
    ctx_vals = lambda: vx_ref[...]
    if not windowed:
        _interleave(scores(lambda: q_ref[...], [ctx_keys], buf_a))
        _interleave(attend([ctx_vals], buf_a, 0))
        return

    j = pl.program_id(1)
    last = pl.num_programs(1) - 1
    cur_lo = lambda: kc_ref[0:tq, :]
    cur_hi = lambda: kc_ref[tq:, :]
    nxt_lo = lambda: kn_ref[0:tq, :]
    nxt_hi = lambda: kn_ref[tq:, :]
    x_keys = [(lambda: kp_ref[...], "prev", j > 0), (cur_lo, None, None), (cur_hi, "next", None), ctx_keys]
    y_keys = [(cur_lo, "prev", None), (cur_hi, None, None), (nxt_lo, "next", j < last), ctx_keys]
    z_keys = [(cur_hi, "prev", None), (nxt_lo, None, None), (nxt_hi, "next", None), ctx_keys]
    x_vals = [lambda: vp_ref[...], lambda: vc_ref[:, 0:tq], lambda: vc_ref[:, tq:], ctx_vals]
    y_vals = [lambda: vc_ref[:, 0:tq], lambda: vc_ref[:, tq:], lambda: vn_ref[:, 0:tq], ctx_vals]

    @pl.when(j == 0)
    def _():
        _interleave(scores(lambda: q_ref[:, 0:tq], x_keys, buf_a))

    _interleave(scores(lambda: q_ref[:, tq:], y_keys, buf_b), attend(x_vals, buf_a, 0))
    _interleave(scores(lambda: qn_ref[...], z_keys, buf_a), attend(y_vals, buf_b, tq))


def _win_attn(qa, ka, va, kac, vac, ga, sink_row, windowed):
    bn, _, t = qa.shape
    ctx_len = kac.shape[1]
    kx_spec = pl.BlockSpec((None, ctx_len, A_KV_WIDTH), lambda b, j: (b, 0, 0))
    vx_spec = pl.BlockSpec((None, A_KV_WIDTH, ctx_len), lambda b, j: (b, 0, 0))
    if windowed:
        tq = Q_BLOCK
        nb = t // tq
        assert nb % 2 == 0
        steps = nb // 2
        n_keys = 3 * tq + ctx_len
        before = lambda j: jnp.maximum(2 * j - 1, 0)
        after = lambda j: jnp.minimum(j + 1, steps - 1)
        pair_spec = pl.BlockSpec((None, A_WIDTH, 2 * tq), lambda b, j: (b, 0, j))
        in_specs = [pair_spec,
                    pl.BlockSpec((None, A_WIDTH, tq), lambda b, j: (b, 0, jnp.minimum(2 * j + 2, nb - 1))),
                    pl.BlockSpec((None, tq, A_KV_WIDTH), lambda b, j: (b, before(j), 0)),
                    pl.BlockSpec((None, 2 * tq, A_KV_WIDTH), lambda b, j: (b, j, 0)),
                    pl.BlockSpec((None, 2 * tq, A_KV_WIDTH), lambda b, j: (b, after(j), 0)),
                    kx_spec,
                    pl.BlockSpec((None, A_KV_WIDTH, tq), lambda b, j: (b, 0, before(j))),
                    pl.BlockSpec((None, A_KV_WIDTH, 2 * tq), lambda b, j: (b, 0, j)),
                    pl.BlockSpec((None, A_KV_WIDTH, 2 * tq), lambda b, j: (b, 0, after(j))),
                    vx_spec, pair_spec]
        args = [qa, qa, ka, ka, ka, kac, va, va, va, vac, ga, sink_row]
        out_spec = pair_spec
    else:
        tq = t
        steps = 1
        n_keys = ctx_len
        out_spec = pl.BlockSpec((None, A_WIDTH, tq), lambda b, j: (b, 0, 0))
        in_specs = [out_spec, kx_spec, vx_spec, out_spec]
        args = [qa, kac, vac, ga, sink_row]
    width = A_HEADS * tq
    return pl.pallas_call(
        functools.partial(_win_attn_kernel, tq=tq, windowed=windowed),
        grid=(bn, steps),
        in_specs=in_specs + [pl.BlockSpec((1, width), lambda b, j: (0, 0))],
        out_specs=out_spec,
        out_shape=jax.ShapeDtypeStruct((bn, A_WIDTH, t), BF16),
        scratch_shapes=[pltpu.VMEM((A_KV_WIDTH, width), BF16), pltpu.VMEM((A_KV_WIDTH, width), BF16),
                        pltpu.VMEM((n_keys, width), F32), pltpu.VMEM((n_keys, width), F32),
                        pltpu.VMEM((1, width), F32), pltpu.VMEM((1, width), F32)],
        compiler_params=_params("arbitrary", "arbitrary"),
        name="win_attn" if windowed else "ctx_sink_attn",
    )(*args)


def _key_chunks(segments, key_chunk):
    chunks = []
    base = 0
    for n, load_k, load_v in segments:
        for c0 in range(0, n, key_chunk):
            size = min(key_chunk, n - c0)
            chunks.append((functools.partial(load_k, c0, size), functools.partial(load_v, c0, size),
                           base + c0, size))
        base += n
    return chunks


def _diff_scores(q, chunks, buf, tq):
    qpad_ref, s_ref, m_ref = buf
    qpad_ref[...] = jnp.zeros_like(qpad_ref)
    qpad_ref[0:HEAD_DIM, 0:tq] = q[0:HEAD_DIM, :]
    qpad_ref[HEAD_DIM:, tq:] = q[HEAD_DIM:, :]
    qpad = qpad_ref[...]
    m = None
    for load_k, _, row, size in chunks:
        s = _dot(load_k(), qpad)
        s_ref[row:row + size, :] = s
        s_max = jnp.max(s, axis=0, keepdims=True)
        m = s_max if m is None else jnp.maximum(m, s_max)
        yield
    m_ref[...] = m


def _diff_attend(chunks, buf, tq, lam, emit):
    _, s_ref, m_ref = buf
    m = m_ref[...]
    l = jnp.zeros((1, 2 * tq), F32)
    acc = jnp.zeros((B_VDIM, 2 * tq), F32)
    for _, load_v, row, size in chunks:
        p = jnp.exp2(s_ref[row:row + size, :] - m)
        l = l + jnp.sum(p, axis=0, keepdims=True)
        p = p.astype(BF16)
        yield
        acc = acc + _dot(load_v(), p)
    acc = acc * (1.0 / l)
    emit(acc[:, 0:tq] - lam * acc[:, tq:])


def _diff_emit(g_ref, w_ref, o_ref, rows, cols):
    def emit(o):
        r = lax.rsqrt(jnp.mean(o * o, axis=0, keepdims=True) + EPS)
        o = (o * r) * w_ref[...]
        o_ref[rows, cols] = (o * _silu(g_ref[rows, cols].astype(F32))).astype(BF16)
    return emit


def _diff_attn_kernel(q_ref, qn_ref, k_ref, kx_ref, kz_ref, kxz_ref, v_ref, vx_ref,
                      g_ref, w_ref, lam_ref, o_ref, qpad_a, qpad_b, s_a, s_b, m_a, m_b,
                      *, tq, key_chunk):
    buf_a = (qpad_a, s_a, m_a)
    buf_b = (qpad_b, s_b, m_b)

    def chunks_of(keys, ctx_keys):
        return _key_chunks(
            [(keys.shape[0], lambda c0, n: keys[c0:c0 + n, :], lambda c0, n: v_ref[:, c0:c0 + n]),
             (ctx_keys.shape[0], lambda c0, n: ctx_keys[c0:c0 + n, :], lambda c0, n: vx_ref[:, c0:c0 + n])],
            key_chunk)

    cur = chunks_of(k_ref, kx_ref)
    ahead = chunks_of(kz_ref, kxz_ref)
    lam = lam_ref[...]
    rows = slice(0, B_VDIM)

    @pl.when((pl.program_id(0) == 0) & (pl.program_id(1) == 0) & (pl.program_id(2) == 0))
    def _():
        _interleave(_diff_scores(q_ref[:, 0:tq], cur, buf_a, tq))

    _interleave(_diff_scores(q_ref[:, tq:], cur, buf_b, tq),
                _diff_attend(cur, buf_a, tq, lam, _diff_emit(g_ref, w_ref, o_ref, rows, slice(0, tq))))
    _interleave(_diff_scores(qn_ref[...], ahead, buf_a, tq),
                _diff_attend(cur, buf_b, tq, lam, _diff_emit(g_ref, w_ref, o_ref, rows, slice(tq, 2 * tq))))


def _diff_attn(qb, kb, vb, kbc, vbc, gb, w_sub, lam_row, tq):
    bn, _, t = qb.shape
    t_keys = kb.shape[1]
    ctx_len = kbc.shape[1]
    assert t % (2 * tq) == 0
    nq = t // (2 * tq)

    def following(b, h):
        b_next = jnp.minimum(b + (h + 1) // B_HEADS, bn - 1)
        return b_next, jnp.where(b + (h + 1) // B_HEADS > bn - 1, h, (h + 1) % B_HEADS)

    def next_tile(b, h, j):
        b_next, h_next = following(b, h)
        wrap = j == nq - 1
        return jnp.where(wrap, b_next, b), jnp.where(wrap, h_next, h), jnp.where(wrap, 0, 2 * (j + 1))

    def keys_ahead(b, h, j):
        b_next, h_next, _ = next_tile(b, h, j)
        return b_next, 0, h_next

    head_fm = lambda width: pl.BlockSpec((None, B_VDIM, width), lambda b, h, j: (b, h, 0))
    pair_spec = pl.BlockSpec((None, B_VDIM, 2 * tq), lambda b, h, j: (b, h, j))
    in_specs = [
        pair_spec,
        pl.BlockSpec((None, B_VDIM, tq), lambda b, h, j: next_tile(b, h, j)),
        pl.BlockSpec((None, t_keys, B_VDIM), lambda b, h, j: (b, 0, h)),
        pl.BlockSpec((None, ctx_len, B_VDIM), lambda b, h, j: (b, 0, h)),
        pl.BlockSpec((None, t_keys, B_VDIM), lambda b, h, j: keys_ahead(b, h, j)),
        pl.BlockSpec((None, ctx_len, B_VDIM), lambda b, h, j: keys_ahead(b, h, j)),
        head_fm(t_keys), head_fm(ctx_len), pair_spec,
        pl.BlockSpec((B_VDIM, tq), lambda b, h, j: (0, 0)),
        pl.BlockSpec((1, tq), lambda b, h, j: (0, 0)),
    ]
    n_keys = t_keys + ctx_len
    return pl.pallas_call(
        functools.partial(_diff_attn_kernel, tq=tq, key_chunk=512),
        grid=(bn, B_HEADS, nq),
        in_specs=in_specs,
        out_specs=pair_spec,
        out_shape=jax.ShapeDtypeStruct((bn, B_WIDTH, t), BF16),
        scratch_shapes=[pltpu.VMEM((B_VDIM, 2 * tq), BF16), pltpu.VMEM((B_VDIM, 2 * tq), BF16),
                        pltpu.VMEM((n_keys, 2 * tq), F32), pltpu.VMEM((n_keys, 2 * tq), F32),
                        pltpu.VMEM((1, 2 * tq), F32), pltpu.VMEM((1, 2 * tq), F32)],
        compiler_params=_params("arbitrary", "arbitrary", "arbitrary"),
        name="diff_attn",
    )(qb, qb, kb, kbc, kb, kbc, vb, vbc, gb, w_sub, lam_row)


def _ctx_diff_attn_kernel(q_ref, kx_ref, vx_ref, g_ref, w_ref, lam_ref, o_ref,
                          qpad_a, qpad_b, s_a, s_b, m_a, m_b, *, tq, key_chunk):
    bufs = [(qpad_a, s_a, m_a), (qpad_b, s_b, m_b)]
    lam = lam_ref[...]
    n = kx_ref.shape[0]

    def head(h):
        rows = slice(B_VDIM * h, B_VDIM * (h + 1))
        chunks = _key_chunks([(n, lambda c0, size: kx_ref[c0:c0 + size, rows],
                               lambda c0, size: vx_ref[rows, c0:c0 + size])], key_chunk)
        return rows, chunks

    rows, chunks = head(0)
    _interleave(_diff_scores(q_ref[rows, :], chunks, bufs[0], tq))
    for h in range(B_HEADS):
        rows, chunks = head(h)
        gens = [_diff_attend(chunks, bufs[h % 2], tq, lam,
                             _diff_emit(g_ref, w_ref, o_ref, rows, slice(0, tq)))]
        if h + 1 < B_HEADS:
            rows_n, chunks_n = head(h + 1)
            gens.insert(0, _diff_scores(q_ref[rows_n, :], chunks_n, bufs[(h + 1) % 2], tq))
        _interleave(*gens)


def _ctx_diff_attn(qbc, kbc, vbc, gbc, w_sub, lam_row):
    bn, _, tq = qbc.shape
    n_keys = kbc.shape[1]
    fm = pl.BlockSpec((None, B_WIDTH, tq), lambda b: (b, 0, 0))
    return pl.pallas_call(
        functools.partial(_ctx_diff_attn_kernel, tq=tq, key_chunk=512),
        grid=(bn,),
        in_specs=[fm, pl.BlockSpec((None, n_keys, B_WIDTH), lambda b: (b, 0, 0)),
                  pl.BlockSpec((None, B_WIDTH, n_keys), lambda b: (b, 0, 0)), fm,
                  pl.BlockSpec((B_VDIM, tq), lambda b: (0, 0)),
                  pl.BlockSpec((1, tq), lambda b: (0, 0))],
        out_specs=fm,
        out_shape=jax.ShapeDtypeStruct((bn, B_WIDTH, tq), BF16),
        scratch_shapes=[pltpu.VMEM((B_VDIM, 2 * tq), BF16), pltpu.VMEM((B_VDIM, 2 * tq), BF16),
                        pltpu.VMEM((n_keys, 2 * tq), F32), pltpu.VMEM((n_keys, 2 * tq), F32),
                        pltpu.VMEM((1, 2 * tq), F32), pltpu.VMEM((1, 2 * tq), F32)],
        compiler_params=_params("arbitrary"),
        name="ctx_diff_attn",
    )(qbc, kbc, vbc, gbc, w_sub, lam_row)


def _out_proj_kernel(*refs, latent):
    (oa_ref, ob_ref, w0_ref, x_ref, gt_ref, g_ref, sc_ref, sh_ref, w_ref, wab_ref, bab_ref) = refs[:11]
    if latent:
        xo_ref, q_ref, k_ref, v_ref, gate_ref, cumf_ref, cumb_ref = refs[11:]
    else:
        k_ref, v_ref, cumf_ref, cumb_ref = refs[11:]
    y = _dot_tn(oa_ref[...], w0_ref[0:A_WIDTH, :]) + _dot_tn(ob_ref[...], w0_ref[A_WIDTH:, :])
    x_new = x_ref[...] + gt_ref[...] * y
    h = _modulate(x_new, g_ref, sc_ref, sh_ref)

    def cols(start, size):
        return _dot(h, w_ref[:, start:start + size])

    if latent:
        xo_ref[...] = x_new
        q_ref[...] = (cols(0, G_QK) * (G_DK ** -0.5)).astype(BF16)
        for i in range(G_V // 512):
            gate_ref[:, 512 * i:512 * (i + 1)] = _silu(cols(2 * G_QK + G_V + 512 * i, 512)).astype(BF16)
    k_ref[...] = cols(G_QK, G_QK).astype(BF16)
    for i in range(G_V // 512):
        v_ref[:, 512 * i:512 * (i + 1)] = cols(2 * G_QK + 512 * i, 512).astype(BF16)
    low = cols(ODD_MAIN, 128).astype(BF16)
    z = _dot(low, wab_ref[...]) + bab_ref[...]
    la = (jnp.minimum(z, 0.0) - jnp.log(1.0 + jnp.exp(-jnp.abs(z)))) / G_TAU
    r = lax.broadcasted_iota(jnp.int32, (G_BLOCK, G_BLOCK), 0)
    c = lax.broadcasted_iota(jnp.int32, (G_BLOCK, G_BLOCK), 1)
    in_chunk = r % G_CHUNK
    tri_f = jnp.where((r - c).astype(jnp.uint32) <= in_chunk.astype(jnp.uint32), 1.0, 0.0)
    tri_b = jnp.where((c - r).astype(jnp.uint32) <= (G_CHUNK - 1 - in_chunk).astype(jnp.uint32),
                      1.0, 0.0)
    for blk in range(la.shape[0] // G_BLOCK):
        rows = slice(blk * G_BLOCK, (blk + 1) * G_BLOCK)
        for tri, lo_col, out_ref in ((tri_f, 0, cumf_ref), (tri_b, G_QK, cumb_ref)):
            hi, lo = _split_bf16(la[rows, lo_col:lo_col + G_QK])
            cum = _dot(tri.astype(BF16), jnp.concatenate([hi, lo], axis=1))
            out_ref[rows, :] = cum[:, :G_QK] + cum[:, G_QK:]


def _out_proj(oa, ob, w0, x, gate, norm_g, sc, sh, w, wab, bab, latent, tm):
    bn, t, _ = x.shape
    fm = pl.BlockSpec((None, A_WIDTH, tm), lambda j, b: (b, 0, j))

    def tk(width, dtype):
        return (pl.BlockSpec((None, tm, width), lambda j, b: (b, j, 0)),
                jax.ShapeDtypeStruct((bn, t, width), dtype))

    const = lambda rows, cols: pl.BlockSpec((rows, cols), lambda j, b: (0, 0))
    x_spec = tk(D_MODEL, F32)
    tail = [tk(G_QK, BF16), tk(G_V, BF16), tk(G_QK, F32), tk(G_QK, F32)]
    if latent:
        outs = [x_spec, tk(G_QK, BF16), tail[0], tail[1], tk(G_V, BF16), tail[2], tail[3]]
    else:
        outs = tail
    return pl.pallas_call(
        functools.partial(_out_proj_kernel, latent=latent),
        grid=(t // tm, bn),
        in_specs=[
            fm, fm, const(D_MODEL, D_MODEL), x_spec[0], _mod_spec(latent),
            const(1, D_MODEL), _mod_spec(latent), _mod_spec(latent),
            const(D_MODEL, ODD_PAD), const(128, 2 * G_QK), const(1, 2 * G_QK),
        ],
        out_specs=[o[0] for o in outs],
        out_shape=[o[1] for o in outs],
        compiler_params=_params("arbitrary", "arbitrary"),
        name="out_proj" if latent else "ctx_out_proj",
    )(oa, ob, w0, x, gate, norm_g, sc, sh, w, wab, bab)


def _gla_block(q, k, v, b, state, reverse, with_out):
    n = G_BLOCK // G_CHUNK
    ends = [(j * G_CHUNK if reverse else (j + 1) * G_CHUNK - 1) for j in range(n)]
    b_last = jnp.concatenate(
        [jnp.broadcast_to(b[e:e + 1, :], (G_CHUNK, G_DK)) for e in ends], axis=0)
    kf = k.astype(F32)
    k_state = (kf * jnp.exp(b_last - b)).astype(BF16)
    if with_out:
        qd = (q.astype(F32) * jnp.exp(b)).astype(BF16)
        kd = (kf * jnp.exp(-b)).astype(BF16)
    yield
    if with_out:
        att = _dot_nt(qd, kd)
        yield
    chunks = [slice(j * G_CHUNK, (j + 1) * G_CHUNK) for j in range(n)]
    d_state = [_dot_tn(k_state[sl], v[sl]) for sl in chunks]
    decay_t = jnp.exp(b_last).T
    yield
    if with_out:
        r = lax.broadcasted_iota(jnp.int32, (G_BLOCK, G_BLOCK), 0)
        c = lax.broadcasted_iota(jnp.int32, (G_BLOCK, G_BLOCK), 1)
        in_chunk = r % G_CHUNK
        if reverse:
            causal = (c - r).astype(jnp.uint32) <= (G_CHUNK - 1 - in_chunk).astype(jnp.uint32)
        else:
            causal = (r - c).astype(jnp.uint32) <= in_chunk.astype(jnp.uint32)
        intra = _dot(jnp.where(causal, att, 0.0).astype(BF16), v)
        yield
    outs = [None] * n
    for j in (reversed(range(n)) if reverse else range(n)):
        if with_out:
            outs[j] = intra[chunks[j]] + _dot(qd[chunks[j]], state.astype(BF16))
        decay = jnp.broadcast_to(decay_t[:, ends[j]:ends[j] + 1], (G_DK, G_DV))
        state = state * decay + d_state[j]
        yield
    return (jnp.concatenate(outs, axis=0) if with_out else None), state


def _gla_kernel(q_ref, k_ref, v_ref, cumf_ref, cumb_ref, g_ref,
                kc_ref, vc_ref, cumfc_ref, cumbc_ref, w_ref, o_ref, of_ref, ob_ref):
    n_lat = q_ref.shape[0] // G_BLOCK
    n_ctx = kc_ref.shape[0] // G_BLOCK

    def rows(i):
        return pl.ds(pl.multiple_of(i * G_BLOCK, G_BLOCK), G_BLOCK)

    st_f = st_b = jnp.zeros((G_DK, G_DV), F32)
    for i in range(n_ctx):
        sl_f = slice(i * G_BLOCK, (i + 1) * G_BLOCK)
        sl_b = slice((n_ctx - 1 - i) * G_BLOCK, (n_ctx - i) * G_BLOCK)
        (_, st_f), (_, st_b) = _interleave(
            _gla_block(None, kc_ref[sl_f, :], vc_ref[sl_f, :], cumfc_ref[sl_f, :], st_f, False, False),
            _gla_block(None, kc_ref[sl_b, :], vc_ref[sl_b, :], cumbc_ref[sl_b, :], st_b, True, False))

    def finish(i, o, other_ref):
        o = o + other_ref[rows(i), :]
        rn = lax.rsqrt(jnp.mean(o * o, axis=-1, keepdims=True) + EPS)
        o = (o * rn) * w_ref[...]
        o_ref[rows(i), :] = (o * g_ref[rows(i), :].astype(F32)).astype(BF16)

    def body(i, states, second_half):
        i_b = n_lat - 1 - i
        (o_f, st_f), (o_b, st_b) = _interleave(
            _gla_block(q_ref[rows(i), :], k_ref[rows(i), :], v_ref[rows(i), :],
                       cumf_ref[rows(i), :], states[0], False, True),
            _gla_block(q_ref[rows(i_b), :], k_ref[rows(i_b), :], v_ref[rows(i_b), :],
                       cumb_ref[rows(i_b), :], states[1], True, True))
        if second_half:
            finish(i, o_f, ob_ref)
            finish(i_b, o_b, of_ref)
        else:
            of_ref[rows(i), :] = o_f
            ob_ref[rows(i_b), :] = o_b
        return st_f, st_b

    assert n_lat % 2 == 0
    states = lax.fori_loop(0, n_lat // 2, functools.partial(body, second_half=False),
                           (st_f, st_b), unroll=2)
    lax.fori_loop(n_lat // 2, n_lat, functools.partial(body, second_half=True), states, unroll=2)


def _gla(q, k, v, laf, lab, gate, kc, vc, lafc, labc, w_norm):
    bn, t, _ = q.shape
    ctx_len = kc.shape[1]
    lat = lambda width: pl.BlockSpec((None, t, width), lambda b, h: (b, 0, h))
    ctx = lambda width: pl.BlockSpec((None, ctx_len, width), lambda b, h: (b, 0, h))
    return pl.pallas_call(
        _gla_kernel,
        grid=(bn, G_HEADS),
        in_specs=[lat(G_DK), lat(G_DK), lat(G_DV), lat(G_DK), lat(G_DK), lat(G_DV),
                  ctx(G_DK), ctx(G_DV), ctx(G_DK), ctx(G_DK),
                  pl.BlockSpec((1, G_DV), lambda b, h: (0, 0))],
        out_specs=lat(G_DV),
        out_shape=jax.ShapeDtypeStruct((bn, t, G_V), BF16),
        scratch_shapes=[pltpu.VMEM((t, G_DV), F32), pltpu.VMEM((t, G_DV), F32)],
        compiler_params=_params("arbitrary", "arbitrary"),
        name="gla",
    )(q, k, v, laf, lab, gate, kc, vc, lafc, labc, w_norm)


def _odd_out_kernel(o_ref, w_ref, x_ref, gt_ref, out_ref):
    out_ref[...] = x_ref[...] + gt_ref[...] * _dot(o_ref[...], w_ref[...])


def _odd_out(o, w, x, gate, tm):
    bn, t, _ = x.shape
    xs = pl.BlockSpec((None, tm, D_MODEL), lambda j, b: (b, j, 0))
    return pl.pallas_call(
        _odd_out_kernel,
        grid=(t // tm, bn),
        in_specs=[xs, pl.BlockSpec((D_MODEL, D_MODEL), lambda j, b: (0, 0)), xs, _mod_spec(True)],
        out_specs=xs,
        out_shape=jax.ShapeDtypeStruct(x.shape, F32),
        compiler_params=_params("arbitrary", "arbitrary"),
        name="odd_out",
    )(o, w, x, gate)


TOKEN_TILE = 1024


def _token_tile(t):
    return TOKEN_TILE if t % TOKEN_TILE == 0 else t


def kernel(x, c, ctx, c_ctx, adaln_w, adaln_b, norm_g, w_out, ab_w_in, a_q_norm, a_k_norm, a_sink,
           b_q_norm, b_k_norm, b_lambda_q1, b_lambda_k1, b_lambda_q2, b_lambda_k2, b_subln,
           gla_w_in, gla_wa_f, gla_ba_f, gla_wa_b, gla_ba_b, gla_out_norm):
    bn, t, _ = x.shape
    ctx_len = ctx.shape[1]
    depth = adaln_w.shape[0]
    assert depth == 2

    pad_rows = (-(bn + 1)) % 8
    cond = jnp.concatenate([c, c_ctx[None, :], jnp.zeros((pad_rows, D_MODEL), F32)], axis=0)
    mod = _adaln(cond, adaln_w, adaln_b[:, None, :])

    def mods(layer):
        m = mod[layer]
        shift, scale, gate = (m[:, i * D_MODEL:(i + 1) * D_MODEL] for i in range(3))
        per_x = tuple(v[:bn, None, :] for v in (shift, scale, gate))
        per_c = tuple(v[bn:bn + 1, None, :] for v in (shift, scale, gate))
        return per_x, per_c

    (shx, scx, gtx), (shc, scc, gtc) = mods(0)
    lambda_init = 0.8 - 0.6 * math.exp(-0.3 * 0)
    wt = ab_w_in[0].T.astype(BF16)
    gains = [a_q_norm[0], a_k_norm[0], b_q_norm[0], b_k_norm[0]]
    q_scale = [ATTN_SCALE * LOG2E, 1.0, ATTN_SCALE * LOG2E, 1.0]
    tab_x = _rope_tables(gains, t, True, q_scale)
    tab_c = _rope_tables(gains, ctx_len, False, q_scale)
    g0 = norm_g[0][None, :]
    qa, ka, va, ga, qb, kb, vb, gb = _even_proj(x, g0, scx, shx, wt, tab_x, True, _token_tile(t))
    qac, kac, vac, gac, qbc, kbc, vbc, gbc = _even_proj(ctx, g0, scc, shc, wt, tab_c, False,
                                                        _token_tile(ctx_len))

    sink = a_sink[0].astype(F32) * LOG2E
    oa = _win_attn(qa, ka, va, kac, vac, ga, jnp.repeat(sink, Q_BLOCK)[None, :], True)
    oac = _win_attn(qac, None, None, kac, vac, gac, jnp.repeat(sink, ctx_len)[None, :], False)

    lam = (jnp.exp(jnp.sum(b_lambda_q1[0].astype(F32) * b_lambda_k1[0].astype(F32)))
           - jnp.exp(jnp.sum(b_lambda_q2[0].astype(F32) * b_lambda_k2[0].astype(F32))) + lambda_init)
    tq = 256
    w_sub = jnp.broadcast_to((b_subln[0].astype(F32) * (1.0 - lambda_init))[:, None], (B_VDIM, tq))
    lam_row = jnp.broadcast_to(lam, (1, tq)).astype(F32)
    ob = _diff_attn(qb, kb, vb, kbc, vbc, gb, w_sub, lam_row, tq)
    obc = _ctx_diff_attn(qbc, kbc, vbc, gbc, w_sub, lam_row)

    w0 = w_out[0].astype(BF16)
    (shx1, scx1, gtx1), (shc1, scc1, _) = mods(1)
    w1 = jnp.pad(gla_w_in[0], ((0, 0), (0, ODD_PAD - gla_w_in.shape[2]))).astype(BF16)
    wab = jnp.zeros((128, 2 * G_QK), F32)
    wab = wab.at[0:G_RANK, 0:G_QK].set(gla_wa_f[0])
    wab = wab.at[G_RANK:2 * G_RANK, G_QK:].set(gla_wa_b[0]).astype(BF16)
    bab = jnp.concatenate([gla_ba_f[0], gla_ba_b[0]])[None, :].astype(F32)
    g1 = norm_g[1][None, :]
    x, q, k, v, gate, cumf, cumb = _out_proj(oa, ob, w0, x, gtx, g1, scx1, shx1, w1, wab, bab,
                                             True, _token_tile(t))
    kc, vc, cumfc, cumbc = _out_proj(oac, obc, w0, ctx, gtc, g1, scc1, shc1, w1, wab, bab,
                                     False, _token_tile(ctx_len))
    o = _gla(q, k, v, cumf, cumb, gate, kc, vc, cumfc, cumbc, gla_out_norm[0][None, :].astype(F32))
    return _odd_out(o, w_out[1].astype(BF16), x, gtx1, _token_tile(t))
```

```python
import functools
import math

import jax
import jax.numpy as jnp
from jax import lax
from jax.experimental import pallas as pl
from jax.experimental.pallas import tpu as pltpu

D_MODEL = 1024
GRID_W = 64
HEAD_DIM = 64
ROPE_BASE = 10000.0
EPS = 1e-6
NEG_INF = -1e30
ATTN_SCALE = HEAD_DIM ** -0.5
LOG2E = math.log2(math.e)
WINDOW = 128
Q_BLOCK = 128

A_HEADS = 8
A_KV_HEADS = 2
A_GROUP = A_HEADS // A_KV_HEADS
A_WIDTH = A_HEADS * HEAD_DIM
A_KV_WIDTH = A_KV_HEADS * HEAD_DIM
B_HEADS = 4
B_VDIM = 2 * HEAD_DIM
B_WIDTH = B_HEADS * B_VDIM
EVEN_IN = 2 * A_WIDTH + 2 * A_KV_WIDTH + 4 * B_WIDTH

G_HEADS = 4
G_DK = 128
G_DV = 256
G_RANK = 16
G_TAU = 16.0
G_CHUNK = 64
G_BLOCK = 256
GLA_HEADS_PER_STEP = 2
G_QK = G_HEADS * G_DK
G_V = G_HEADS * G_DV
ODD_MAIN = 2 * G_QK + 2 * G_V
ODD_PAD = ODD_MAIN + 128

VMEM_LIMIT_BYTES = 56 * 1024 * 1024

BF16 = jnp.bfloat16
F32 = jnp.float32


def _params(*semantics):
    return pltpu.CompilerParams(dimension_semantics=semantics,
                                vmem_limit_bytes=VMEM_LIMIT_BYTES)


def _silu(v):
    return v * (1.0 / (1.0 + jnp.exp(-v)))


def _dot(a, b):
    return jnp.dot(a, b, preferred_element_type=F32)


def _dot_nt(a, b):
    return lax.dot_general(a, b, (((1,), (1,)), ((), ())), preferred_element_type=F32)


def _dot_tn(a, b):
    return lax.dot_general(a, b, (((0,), (0,)), ((), ())), preferred_element_type=F32)


def _split_bf16(v):
    hi = v.astype(BF16)
    lo = (v - hi.astype(F32)).astype(BF16)
    return hi, lo


def _adaln_kernel(cond_ref, w_ref, b_ref, o_ref):
    a = _silu(cond_ref[...])
    a_hi, a_lo = _split_bf16(a)
    w_hi, w_lo = _split_bf16(w_ref[...])
    o_ref[...] = (_dot(a_hi, w_hi) + _dot(a_hi, w_lo) + _dot(a_lo, w_hi)) + b_ref[...]


def _adaln(cond, w, b):
    n_layers = w.shape[0]
    rows = cond.shape[0]
    tn = 512
    return pl.pallas_call(
        _adaln_kernel,
        grid=(n_layers, 3 * D_MODEL // tn),
        in_specs=[
            pl.BlockSpec((rows, D_MODEL), lambda l, n: (0, 0)),
            pl.BlockSpec((None, D_MODEL, tn), lambda l, n: (l, 0, n)),
            pl.BlockSpec((None, 1, tn), lambda l, n: (l, 0, n)),
        ],
        out_specs=pl.BlockSpec((None, rows, tn), lambda l, n: (l, 0, n)),
        out_shape=jax.ShapeDtypeStruct((n_layers, rows, 3 * D_MODEL), F32),
        compiler_params=_params("arbitrary", "arbitrary"),
        name="adaln",
    )(cond, w, b)


def _modulate(xf, g_ref, sc_ref, sh_ref):
    r = lax.rsqrt(jnp.mean(xf * xf, axis=-1, keepdims=True) + EPS)
    return ((xf * r) * g_ref[...] * (1.0 + sc_ref[...]) + sh_ref[...]).astype(BF16)


def _modulated(x_ref, g_ref, sc_ref, sh_ref):
    return _modulate(x_ref[...], g_ref, sc_ref, sh_ref)


def _interleave(*gens):
    results = [None] * len(gens)
    live = list(range(len(gens)))
    while live:
        for idx in list(live):
            try:
                next(gens[idx])
            except StopIteration as stop:
                results[idx] = stop.value
                live.remove(idx)
    return results


def _mod_spec(per_batch):
    if per_batch:
        return pl.BlockSpec((None, 1, D_MODEL), lambda j, b: (b, 0, 0))
    return pl.BlockSpec((None, 1, D_MODEL), lambda j, b: (0, 0, 0))


def _rope_head(blk, c, s):
    r = lax.rsqrt(jnp.mean(blk * blk, axis=0, keepdims=True) + EPS)
    partner = jnp.concatenate([blk[16:32], blk[0:16], blk[48:64], blk[32:48]], axis=0)
    return (blk * c + partner * s) * r


def _even_proj_kernel(x_ref, g_ref, sc_ref, sh_ref, wt_ref, tab_ref,
                      qa_ref, ka_ref, va_ref, ga_ref, qb_ref, kb_ref, vb_ref, gb_ref):
    h = _modulated(x_ref, g_ref, sc_ref, sh_ref)

    def rows(start, size):
        return _dot_nt(wt_ref[start:start + size, :], h)

    def normed(acc, table):
        c = tab_ref[2 * table]
        s = tab_ref[2 * table + 1]
        n = acc.shape[0] // HEAD_DIM
        return jnp.concatenate(
            [_rope_head(acc[HEAD_DIM * i:HEAD_DIM * (i + 1)], c, s) for i in range(n)], axis=0)

    off = 0
    qa_ref[...] = normed(rows(off, A_WIDTH), 0).astype(BF16)
    off += A_WIDTH
    kv = rows(off, 2 * A_KV_WIDTH)
    ka_ref[...] = normed(kv[0:A_KV_WIDTH], 1).T.astype(BF16)
    va_ref[...] = kv[A_KV_WIDTH:].astype(BF16)
    off += 2 * A_KV_WIDTH
    ga_ref[...] = rows(off, A_WIDTH).astype(BF16)
    off += A_WIDTH
    qb_ref[...] = normed(rows(off, B_WIDTH), 2).astype(BF16)
    off += B_WIDTH
    kb_ref[...] = normed(rows(off, B_WIDTH), 3).T.astype(BF16)
    off += B_WIDTH
    vb_ref[...] = rows(off, B_WIDTH).astype(BF16)
    off += B_WIDTH
    gb_ref[...] = rows(off, B_WIDTH).astype(BF16)


def _even_proj(x, norm_g, sc, sh, wt, tables, per_batch, tm):
    bn, t, _ = x.shape
    nt = t // tm

    def fm(width):
        return (pl.BlockSpec((None, width, tm), lambda j, b: (b, 0, j)),
                jax.ShapeDtypeStruct((bn, width, t), BF16))

    def tk(width):
        return (pl.BlockSpec((None, tm, width), lambda j, b: (b, j, 0)),
                jax.ShapeDtypeStruct((bn, t, width), BF16))

    outs = [fm(A_WIDTH), tk(A_KV_WIDTH), fm(A_KV_WIDTH), fm(A_WIDTH),
            fm(B_WIDTH), tk(B_WIDTH), fm(B_WIDTH), fm(B_WIDTH)]
    return pl.pallas_call(
        _even_proj_kernel,
        grid=(nt, bn),
        in_specs=[
            pl.BlockSpec((None, tm, D_MODEL), lambda j, b: (b, j, 0)),
            pl.BlockSpec((1, D_MODEL), lambda j, b: (0, 0)),
            _mod_spec(per_batch), _mod_spec(per_batch),
            pl.BlockSpec((EVEN_IN, D_MODEL), lambda j, b: (0, 0)),
            pl.BlockSpec((8, HEAD_DIM, tm), lambda j, b: (0, 0, j)),
        ],
        out_specs=[o[0] for o in outs],
        out_shape=[o[1] for o in outs],
        compiler_params=_params("arbitrary", "arbitrary"),
        name="even_proj",
    )(x, norm_g, sc, sh, wt, tables)


def _rope_tables(gains, t, rotary, q_scale):
    m = HEAD_DIM // 4
    inv = ROPE_BASE ** (-jnp.arange(m, dtype=F32) / m)
    pos = jnp.arange(t, dtype=jnp.int32)
    rows = (pos // GRID_W).astype(F32)
    cols = (pos % GRID_W).astype(F32)
    if rotary:
        ang_r = inv[:, None] * rows[None, :]
        ang_c = inv[:, None] * cols[None, :]
        cos = jnp.concatenate([jnp.cos(ang_r), jnp.cos(ang_r), jnp.cos(ang_c), jnp.cos(ang_c)], axis=0)
        sin = jnp.concatenate([-jnp.sin(ang_r), jnp.sin(ang_r), -jnp.sin(ang_c), jnp.sin(ang_c)], axis=0)
    else:
        cos = jnp.ones((HEAD_DIM, t), F32)
        sin = jnp.zeros((HEAD_DIM, t), F32)
    out = []
    for g, scale in zip(gains, q_scale):
        g = g.astype(F32)
        gp = jnp.concatenate([g[16:32], g[0:16], g[48:64], g[32:48]])
        out.append(g[:, None] * cos * scale)
        out.append(gp[:, None] * sin * scale)
    return jnp.stack(out)


def _win_attn_kernel(*refs, tq, windowed):
    width = A_HEADS * tq
    if windowed:
        (q_ref, qn_ref, kp_ref, kc_ref, kn_ref, kx_ref, vp_ref, vc_ref, vn_ref, vx_ref,
         g_ref, sink_ref, o_ref, qpad_a, qpad_b, s_a, s_b, m_a, m_b) = refs
    else:
        (q_ref, kx_ref, vx_ref, g_ref, sink_ref, o_ref, qpad_a, qpad_b, s_a, s_b, m_a, m_b) = refs
    buf_a = (qpad_a, s_a, m_a)
    buf_b = (qpad_b, s_b, m_b)

    def scores(load_q, key_segs, buf):
        qpad_ref, s_ref, m_ref = buf
        q = load_q()
        qpad_ref[...] = jnp.zeros_like(qpad_ref)
        for hd in range(A_HEADS):
            grp = hd // A_GROUP
            qpad_ref[HEAD_DIM * grp:HEAD_DIM * (grp + 1), tq * hd:tq * (hd + 1)] = (
                q[HEAD_DIM * hd:HEAD_DIM * (hd + 1), :])
        qpad = qpad_ref[...]
        m = sink_ref[...]
        row = 0
        for load_key, kind, valid in key_segs:
            s = _dot(load_key(), qpad)
            n = s.shape[0]
            if kind is not None:
                r = lax.broadcasted_iota(jnp.int32, s.shape, 0)
                cq = lax.broadcasted_iota(jnp.int32, s.shape, 1) % tq
                off = 0 if valid is None else jnp.where(valid, 0, tq)
                ok = (r >= cq + off) if kind == "prev" else (r <= cq - off)
                s = jnp.where(ok, s, NEG_INF)
            s_ref[row:row + n, :] = s
            m = jnp.maximum(m, jnp.max(s, axis=0, keepdims=True))
            row += n
            yield
        m_ref[...] = m

    def attend(value_segs, buf, col0):
        _, s_ref, m_ref = buf
        m = m_ref[...]
        l = jnp.exp2(sink_ref[...] - m)
        acc = jnp.zeros((A_KV_WIDTH, width), F32)
        row = 0
        for load_value in value_segs:
            v = load_value()
            n = v.shape[1]
            p = jnp.exp2(s_ref[row:row + n, :] - m)
            l = l + jnp.sum(p, axis=0, keepdims=True)
            p = p.astype(BF16)
            yield
            acc = acc + _dot(v, p)
            row += n
        inv = 1.0 / l
        for hd in range(A_HEADS):
            grp = hd // A_GROUP
            o = acc[HEAD_DIM * grp:HEAD_DIM * (grp + 1), tq * hd:tq * (hd + 1)]
            o = o * inv[:, tq * hd:tq * (hd + 1)]
            gate = g_ref[HEAD_DIM * hd:HEAD_DIM * (hd + 1), col0:col0 + tq].astype(F32)
            o_ref[HEAD_DIM * hd:HEAD_DIM * (hd + 1), col0:col0 + tq] = (o * _silu(gate)).astype(BF16)

    ctx_keys = (lambda: kx_ref[...], None, None)
    ctx_vals = lambda: vx_ref[...]
    if not windowed:
        _interleave(scores(lambda: q_ref[...], [ctx_keys], buf_a))
        _interleave(attend([ctx_vals], buf_a, 0))
        return

    j = pl.program_id(1)
    last = pl.num_programs(1) - 1
    cur_lo = lambda: kc_ref[0:tq, :]
    cur_hi = lambda: kc_ref[tq:, :]
    nxt_lo = lambda: kn_ref[0:tq, :]
    nxt_hi = lambda: kn_ref[tq:, :]
    x_keys = [(lambda: kp_ref[...], "prev", j > 0), (cur_lo, None, None), (cur_hi, "next", None), ctx_keys]
    y_keys = [(cur_lo, "prev", None), (cur_hi, None, None), (nxt_lo, "next", j < last), ctx_keys]
    z_keys = [(cur_hi, "prev", None), (nxt_lo, None, None), (nxt_hi, "next", None), ctx_keys]
    x_vals = [lambda: vp_ref[...], lambda: vc_ref[:, 0:tq], lambda: vc_ref[:, tq:], ctx_vals]
    y_vals = [lambda: vc_ref[:, 0:tq], lambda: vc_ref[:, tq:], lambda: vn_ref[:, 0:tq], ctx_vals]

    @pl.when(j == 0)
    def _():
        _interleave(scores(lambda: q_ref[:, 0:tq], x_keys, buf_a))

    _interleave(scores(lambda: q_ref[:, tq:], y_keys, buf_b), attend(x_vals, buf_a, 0))
    _interleave(scores(lambda: qn_ref[...], z_keys, buf_a), attend(y_vals, buf_b, tq))


def _win_attn(qa, ka, va, kac, vac, ga, sink_row, windowed):
    bn, _, t = qa.shape
    ctx_len = kac.shape[1]
    kx_spec = pl.BlockSpec((None, ctx_len, A_KV_WIDTH), lambda b, j: (b, 0, 0))
    vx_spec = pl.BlockSpec((None, A_KV_WIDTH, ctx_len), lambda b, j: (b, 0, 0))
    if windowed:
        tq = Q_BLOCK
        nb = t // tq
        assert nb % 2 == 0
        steps = nb // 2
        n_keys = 3 * tq + ctx_len
        before = lambda j: jnp.maximum(2 * j - 1, 0)
        after = lambda j: jnp.minimum(j + 1, steps - 1)
        pair_spec = pl.BlockSpec((None, A_WIDTH, 2 * tq), lambda b, j: (b, 0, j))
        in_specs = [pair_spec,
                    pl.BlockSpec((None, A_WIDTH, tq), lambda b, j: (b, 0, jnp.minimum(2 * j + 2, nb - 1))),
                    pl.BlockSpec((None, tq, A_KV_WIDTH), lambda b, j: (b, before(j), 0)),
                    pl.BlockSpec((None, 2 * tq, A_KV_WIDTH), lambda b, j: (b, j, 0)),
                    pl.BlockSpec((None, 2 * tq, A_KV_WIDTH), lambda b, j: (b, after(j), 0)),
                    kx_spec,
                    pl.BlockSpec((None, A_KV_WIDTH, tq), lambda b, j: (b, 0, before(j))),
                    pl.BlockSpec((None, A_KV_WIDTH, 2 * tq), lambda b, j: (b, 0, j)),
                    pl.BlockSpec((None, A_KV_WIDTH, 2 * tq), lambda b, j: (b, 0, after(j))),
                    vx_spec, pair_spec]
        args = [qa, qa, ka, ka, ka, kac, va, va, va, vac, ga, sink_row]
        out_spec = pair_spec
    else:
        tq = t
        steps = 1
        n_keys = ctx_len
        out_spec = pl.BlockSpec((None, A_WIDTH, tq), lambda b, j: (b, 0, 0))
        in_specs = [out_spec, kx_spec, vx_spec, out_spec]
        args = [qa, kac, vac, ga, sink_row]
    width = A_HEADS * tq
    return pl.pallas_call(
        functools.partial(_win_attn_kernel, tq=tq, windowed=windowed),
        grid=(bn, steps),
        in_specs=in_specs + [pl.BlockSpec((1, width), lambda b, j: (0, 0))],
        out_specs=out_spec,
        out_shape=jax.ShapeDtypeStruct((bn, A_WIDTH, t), BF16),
        scratch_shapes=[pltpu.VMEM((A_KV_WIDTH, width), BF16), pltpu.VMEM((A_KV_WIDTH, width), BF16),
                        pltpu.VMEM((n_keys, width), F32), pltpu.VMEM((n_keys, width), F32),
                        pltpu.VMEM((1, width), F32), pltpu.VMEM((1, width), F32)],
        compiler_params=_params("arbitrary", "arbitrary"),
        name="win_attn" if windowed else "ctx_sink_attn",
    )(*args)


def _key_chunks(segments, key_chunk):
    chunks = []
    base = 0
    for n, load_k, load_v in segments:
        for c0 in range(0, n, key_chunk):
            size = min(key_chunk, n - c0)
            chunks.append((functools.partial(load_k, c0, size), functools.partial(load_v, c0, size),
                           base + c0, size))
        base += n
    return chunks


def _diff_scores(q, chunks, buf, tq):
    qpad_ref, s_ref, m_ref = buf
    qpad_ref[...] = jnp.zeros_like(qpad_ref)
    qpad_ref[0:HEAD_DIM, 0:tq] = q[0:HEAD_DIM, :]
    qpad_ref[HEAD_DIM:, tq:] = q[HEAD_DIM:, :]
    qpad = qpad_ref[...]
    m = None
    for load_k, _, row, size in chunks:
        s = _dot(load_k(), qpad)
        s_ref[row:row + size, :] = s
        s_max = jnp.max(s, axis=0, keepdims=True)
        m = s_max if m is None else jnp.maximum(m, s_max)
        yield
    m_ref[...] = m


def _diff_attend(chunks, buf, tq, lam, emit):
    _, s_ref, m_ref = buf
    m = m_ref[...]
    l = jnp.zeros((1, 2 * tq), F32)
    acc = jnp.zeros((B_VDIM, 2 * tq), F32)
    for _, load_v, row, size in chunks:
        p = jnp.exp2(s_ref[row:row + size, :] - m)
        l = l + jnp.sum(p, axis=0, keepdims=True)
        p = p.astype(BF16)
        yield
        acc = acc + _dot(load_v(), p)
    acc = acc * (1.0 / l)
    emit(acc[:, 0:tq] - lam * acc[:, tq:])


def _diff_emit(g_ref, w_ref, o_ref, rows, cols):
    def emit(o):
        r = lax.rsqrt(jnp.mean(o * o, axis=0, keepdims=True) + EPS)
        o = (o * r) * w_ref[...]
        o_ref[rows, cols] = (o * _silu(g_ref[rows, cols].astype(F32))).astype(BF16)
    return emit


def _diff_attn_kernel(q_ref, qn_ref, k_ref, kx_ref, kz_ref, kxz_ref, v_ref, vx_ref,
                      g_ref, w_ref, lam_ref, o_ref, qpad_a, qpad_b, s_a, s_b, m_a, m_b,
                      *, tq, key_chunk):
    buf_a = (qpad_a, s_a, m_a)
    buf_b = (qpad_b, s_b, m_b)

    def chunks_of(keys, ctx_keys):
        return _key_chunks(
            [(keys.shape[0], lambda c0, n: keys[c0:c0 + n, :], lambda c0, n: v_ref[:, c0:c0 + n]),
             (ctx_keys.shape[0], lambda c0, n: ctx_keys[c0:c0 + n, :], lambda c0, n: vx_ref[:, c0:c0 + n])],
            key_chunk)

    cur = chunks_of(k_ref, kx_ref)
    ahead = chunks_of(kz_ref, kxz_ref)
    lam = lam_ref[...]
    rows = slice(0, B_VDIM)

    @pl.when((pl.program_id(0) == 0) & (pl.program_id(1) == 0) & (pl.program_id(2) == 0))
    def _():
        _interleave(_diff_scores(q_ref[:, 0:tq], cur, buf_a, tq))

    _interleave(_diff_scores(q_ref[:, tq:], cur, buf_b, tq),
                _diff_attend(cur, buf_a, tq, lam, _diff_emit(g_ref, w_ref, o_ref, rows, slice(0, tq))))
    _interleave(_diff_scores(qn_ref[...], ahead, buf_a, tq),
                _diff_attend(cur, buf_b, tq, lam, _diff_emit(g_ref, w_ref, o_ref, rows, slice(tq, 2 * tq))))


def _diff_attn(qb, kb, vb, kbc, vbc, gb, w_sub, lam_row, tq):
    bn, _, t = qb.shape
    t_keys = kb.shape[1]
    ctx_len = kbc.shape[1]
    assert t % (2 * tq) == 0
    nq = t // (2 * tq)

    def following(b, h):
        b_next = jnp.minimum(b + (h + 1) // B_HEADS, bn - 1)
        return b_next, jnp.where(b + (h + 1) // B_HEADS > bn - 1, h, (h + 1) % B_HEADS)

    def next_tile(b, h, j):
        b_next, h_next = following(b, h)
        wrap = j == nq - 1
        return jnp.where(wrap, b_next, b), jnp.where(wrap, h_next, h), jnp.where(wrap, 0, 2 * (j + 1))

    def keys_ahead(b, h, j):
        b_next, h_next, _ = next_tile(b, h, j)
        return b_next, 0, h_next

    head_fm = lambda width: pl.BlockSpec((None, B_VDIM, width), lambda b, h, j: (b, h, 0))
    pair_spec = pl.BlockSpec((None, B_VDIM, 2 * tq), lambda b, h, j: (b, h, j))
    in_specs = [
        pair_spec,
        pl.BlockSpec((None, B_VDIM, tq), lambda b, h, j: next_tile(b, h, j)),
        pl.BlockSpec((None, t_keys, B_VDIM), lambda b, h, j: (b, 0, h)),
        pl.BlockSpec((None, ctx_len, B_VDIM), lambda b, h, j: (b, 0, h)),
        pl.BlockSpec((None, t_keys, B_VDIM), lambda b, h, j: keys_ahead(b, h, j)),
        pl.BlockSpec((None, ctx_len, B_VDIM), lambda b, h, j: keys_ahead(b, h, j)),
        head_fm(t_keys), head_fm(ctx_len), pair_spec,
        pl.BlockSpec((B_VDIM, tq), lambda b, h, j: (0, 0)),
        pl.BlockSpec((1, tq), lambda b, h, j: (0, 0)),
    ]
    n_keys = t_keys + ctx_len
    return pl.pallas_call(
        functools.partial(_diff_attn_kernel, tq=tq, key_chunk=512),
        grid=(bn, B_HEADS, nq),
        in_specs=in_specs,
        out_specs=pair_spec,
        out_shape=jax.ShapeDtypeStruct((bn, B_WIDTH, t), BF16),
        scratch_shapes=[pltpu.VMEM((B_VDIM, 2 * tq), BF16), pltpu.VMEM((B_VDIM, 2 * tq), BF16),
                        pltpu.VMEM((n_keys, 2 * tq), F32), pltpu.VMEM((n_keys, 2 * tq), F32),
                        pltpu.VMEM((1, 2 * tq), F32), pltpu.VMEM((1, 2 * tq), F32)],
        compiler_params=_params("arbitrary", "arbitrary", "arbitrary"),
        name="diff_attn",
    )(qb, qb, kb, kbc, kb, kbc, vb, vbc, gb, w_sub, lam_row)


def _ctx_diff_attn_kernel(q_ref, kx_ref, vx_ref, g_ref, w_ref, lam_ref, o_ref,
                          qpad_a, qpad_b, s_a, s_b, m_a, m_b, *, tq, key_chunk):
    bufs = [(qpad_a, s_a, m_a), (qpad_b, s_b, m_b)]
    lam = lam_ref[...]
    n = kx_ref.shape[0]

    def head(h):
        rows = slice(B_VDIM * h, B_VDIM * (h + 1))
        chunks = _key_chunks([(n, lambda c0, size: kx_ref[c0:c0 + size, rows],
                               lambda c0, size: vx_ref[rows, c0:c0 + size])], key_chunk)
        return rows, chunks

    rows, chunks = head(0)
    _interleave(_diff_scores(q_ref[rows, :], chunks, bufs[0], tq))
    for h in range(B_HEADS):
        rows, chunks = head(h)
        gens = [_diff_attend(chunks, bufs[h % 2], tq, lam,
                             _diff_emit(g_ref, w_ref, o_ref, rows, slice(0, tq)))]
        if h + 1 < B_HEADS:
            rows_n, chunks_n = head(h + 1)
            gens.insert(0, _diff_scores(q_ref[rows_n, :], chunks_n, bufs[(h + 1) % 2], tq))
        _interleave(*gens)


def _ctx_diff_attn(qbc, kbc, vbc, gbc, w_sub, lam_row):
    bn, _, tq = qbc.shape
    n_keys = kbc.shape[1]
    fm = pl.BlockSpec((None, B_WIDTH, tq), lambda b: (b, 0, 0))
    return pl.pallas_call(
        functools.partial(_ctx_diff_attn_kernel, tq=tq, key_chunk=512),
        grid=(bn,),
        in_specs=[fm, pl.BlockSpec((None, n_keys, B_WIDTH), lambda b: (b, 0, 0)),
                  pl.BlockSpec((None, B_WIDTH, n_keys), lambda b: (b, 0, 0)), fm,
                  pl.BlockSpec((B_VDIM, tq), lambda b: (0, 0)),
                  pl.BlockSpec((1, tq), lambda b: (0, 0))],
        out_specs=fm,
        out_shape=jax.ShapeDtypeStruct((bn, B_WIDTH, tq), BF16),
        scratch_shapes=[pltpu.VMEM((B_VDIM, 2 * tq), BF16), pltpu.VMEM((B_VDIM, 2 * tq), BF16),
                        pltpu.VMEM((n_keys, 2 * tq), F32), pltpu.VMEM((n_keys, 2 * tq), F32),
                        pltpu.VMEM((1, 2 * tq), F32), pltpu.VMEM((1, 2 * tq), F32)],
        compiler_params=_params("arbitrary"),
        name="ctx_diff_attn",
    )(qbc, kbc, vbc, gbc, w_sub, lam_row)


def _out_proj_kernel(*refs, latent):
    (oa_ref, ob_ref, w0_ref, x_ref, gt_ref, g_ref, sc_ref, sh_ref, w_ref, wab_ref, bab_ref) = refs[:11]
    if latent:
        xo_ref, q_ref, k_ref, v_ref, gate_ref, cumf_ref, cumb_ref = refs[11:]
    else:
        k_ref, v_ref, cumf_ref, cumb_ref = refs[11:]
    y = _dot_tn(oa_ref[...], w0_ref[0:A_WIDTH, :]) + _dot_tn(ob_ref[...], w0_ref[A_WIDTH:, :])
    x_new = x_ref[...] + gt_ref[...] * y
    h = _modulate(x_new, g_ref, sc_ref, sh_ref)

    def cols(start, size):
        return _dot(h, w_ref[:, start:start + size])

    if latent:
        xo_ref[...] = x_new
        q_ref[...] = (cols(0, G_QK) * (G_DK ** -0.5)).astype(BF16)
        for i in range(G_V // 512):
            gate_ref[:, 512 * i:512 * (i + 1)] = _silu(cols(2 * G_QK + G_V + 512 * i, 512)).astype(BF16)
    k_ref[...] = cols(G_QK, G_QK).astype(BF16)
    for i in range(G_V // 512):
        v_ref[:, 512 * i:512 * (i + 1)] = cols(2 * G_QK + 512 * i, 512).astype(BF16)
    low = cols(ODD_MAIN, 128).astype(BF16)
    z = _dot(low, wab_ref[...]) + bab_ref[...]
    la = (jnp.minimum(z, 0.0) - jnp.log(1.0 + jnp.exp(-jnp.abs(z)))) / G_TAU
    r = lax.broadcasted_iota(jnp.int32, (G_BLOCK, G_BLOCK), 0)
    c = lax.broadcasted_iota(jnp.int32, (G_BLOCK, G_BLOCK), 1)
    in_chunk = r % G_CHUNK
    tri_f = jnp.where((r - c).astype(jnp.uint32) <= in_chunk.astype(jnp.uint32), 1.0, 0.0)
    tri_b = jnp.where((c - r).astype(jnp.uint32) <= (G_CHUNK - 1 - in_chunk).astype(jnp.uint32),
                      1.0, 0.0)
    for blk in range(la.shape[0] // G_BLOCK):
        rows = slice(blk * G_BLOCK, (blk + 1) * G_BLOCK)
        for tri, lo_col, out_ref in ((tri_f, 0, cumf_ref), (tri_b, G_QK, cumb_ref)):
            hi, lo = _split_bf16(la[rows, lo_col:lo_col + G_QK])
            cum = _dot(tri.astype(BF16), jnp.concatenate([hi, lo], axis=1))
            out_ref[rows, :] = cum[:, :G_QK] + cum[:, G_QK:]


def _out_proj(oa, ob, w0, x, gate, norm_g, sc, sh, w, wab, bab, latent, tm):
    bn, t, _ = x.shape
    fm = pl.BlockSpec((None, A_WIDTH, tm), lambda j, b: (b, 0, j))

    def tk(width, dtype):
        return (pl.BlockSpec((None, tm, width), lambda j, b: (b, j, 0)),
                jax.ShapeDtypeStruct((bn, t, width), dtype))

    const = lambda rows, cols: pl.BlockSpec((rows, cols), lambda j, b: (0, 0))
    x_spec = tk(D_MODEL, F32)
    tail = [tk(G_QK, BF16), tk(G_V, BF16), tk(G_QK, F32), tk(G_QK, F32)]
    if latent:
        outs = [x_spec, tk(G_QK, BF16), tail[0], tail[1], tk(G_V, BF16), tail[2], tail[3]]
    else:
        outs = tail
    return pl.pallas_call(
        functools.partial(_out_proj_kernel, latent=latent),
        grid=(t // tm, bn),
        in_specs=[
            fm, fm, const(D_MODEL, D_MODEL), x_spec[0], _mod_spec(latent),
            const(1, D_MODEL), _mod_spec(latent), _mod_spec(latent),
            const(D_MODEL, ODD_PAD), const(128, 2 * G_QK), const(1, 2 * G_QK),
        ],
        out_specs=[o[0] for o in outs],
        out_shape=[o[1] for o in outs],
        compiler_params=_params("arbitrary", "arbitrary"),
        name="out_proj" if latent else "ctx_out_proj",
    )(oa, ob, w0, x, gate, norm_g, sc, sh, w, wab, bab)


def _gla_block(q, k, v, b, state, reverse, with_out):
    n = G_BLOCK // G_CHUNK
    ends = [(j * G_CHUNK if reverse else (j + 1) * G_CHUNK - 1) for j in range(n)]
    b_last = jnp.concatenate(
        [jnp.broadcast_to(b[e:e + 1, :], (G_CHUNK, G_DK)) for e in ends], axis=0)
    kf = k.astype(F32)
    k_state = (kf * jnp.exp(b_last - b)).astype(BF16)
    if with_out:
        qd = (q.astype(F32) * jnp.exp(b)).astype(BF16)
        kd = (kf * jnp.exp(-b)).astype(BF16)
    yield
    if with_out:
        att = _dot_nt(qd, kd)
        yield
    chunks = [slice(j * G_CHUNK, (j + 1) * G_CHUNK) for j in range(n)]
    d_state = [_dot_tn(k_state[sl], v[sl]) for sl in chunks]
    decay_t = jnp.exp(b_last).T
    yield
    if with_out:
        r = lax.broadcasted_iota(jnp.int32, (G_BLOCK, G_BLOCK), 0)
        c = lax.broadcasted_iota(jnp.int32, (G_BLOCK, G_BLOCK), 1)
        in_chunk = r % G_CHUNK
        if reverse:
            causal = (c - r).astype(jnp.uint32) <= (G_CHUNK - 1 - in_chunk).astype(jnp.uint32)
        else:
            causal = (r - c).astype(jnp.uint32) <= in_chunk.astype(jnp.uint32)
        intra = _dot(jnp.where(causal, att, 0.0).astype(BF16), v)
        yield
    outs = [None] * n
    for j in (reversed(range(n)) if reverse else range(n)):
        if with_out:
            outs[j] = intra[chunks[j]] + _dot(qd[chunks[j]], state.astype(BF16))
        decay = jnp.broadcast_to(decay_t[:, ends[j]:ends[j] + 1], (G_DK, G_DV))
        state = state * decay + d_state[j]
        yield
    return (jnp.concatenate(outs, axis=0) if with_out else None), state


def _gla_kernel(q_ref, k_ref, v_ref, cumf_ref, cumb_ref, g_ref,
                kc_ref, vc_ref, cumfc_ref, cumbc_ref, w_ref, o_ref, of_ref, ob_ref, *, heads):
    n_lat = q_ref.shape[0] // G_BLOCK
    n_ctx = kc_ref.shape[0] // G_BLOCK
    qk = [slice(G_DK * h, G_DK * (h + 1)) for h in range(heads)]
    vv = [slice(G_DV * h, G_DV * (h + 1)) for h in range(heads)]

    def rows(i):
        return pl.ds(pl.multiple_of(i * G_BLOCK, G_BLOCK), G_BLOCK)

    states = [jnp.zeros((G_DK, G_DV), F32)] * (2 * heads)
    for i in range(n_ctx):
        sl_f = slice(i * G_BLOCK, (i + 1) * G_BLOCK)
        sl_b = slice((n_ctx - 1 - i) * G_BLOCK, (n_ctx - i) * G_BLOCK)
        gens = []
        for h in range(heads):
            gens.append(_gla_block(None, kc_ref[sl_f, qk[h]], vc_ref[sl_f, vv[h]],
                                   cumfc_ref[sl_f, qk[h]], states[2 * h], False, False))
            gens.append(_gla_block(None, kc_ref[sl_b, qk[h]], vc_ref[sl_b, vv[h]],
                                   cumbc_ref[sl_b, qk[h]], states[2 * h + 1], True, False))
        states = [st for _, st in _interleave(*gens)]

    def finish(i, h, o, other_ref):
        o = o + other_ref[rows(i), vv[h]]
        rn = lax.rsqrt(jnp.mean(o * o, axis=-1, keepdims=True) + EPS)
        o = (o * rn) * w_ref[...]
        o_ref[rows(i), vv[h]] = (o * g_ref[rows(i), vv[h]].astype(F32)).astype(BF16)

    def body(i, states, second_half):
        i_b = n_lat - 1 - i
        gens = []
        for h in range(heads):
            gens.append(_gla_block(q_ref[rows(i), qk[h]], k_ref[rows(i), qk[h]], v_ref[rows(i), vv[h]],
                                   cumf_ref[rows(i), qk[h]], states[2 * h], False, True))
            gens.append(_gla_block(q_ref[rows(i_b), qk[h]], k_ref[rows(i_b), qk[h]],
                                   v_ref[rows(i_b), vv[h]], cumb_ref[rows(i_b), qk[h]],
                                   states[2 * h + 1], True, True))
        results = _interleave(*gens)
        for h in range(heads):
            (o_f, _), (o_b, _) = results[2 * h], results[2 * h + 1]
            if second_half:
                finish(i, h, o_f, ob_ref)
                finish(i_b, h, o_b, of_ref)
            else:
                of_ref[rows(i), vv[h]] = o_f
                ob_ref[rows(i_b), vv[h]] = o_b
        return tuple(st for _, st in results)

    assert n_lat % 2 == 0
    states = lax.fori_loop(0, n_lat // 2, functools.partial(body, second_half=False),
                           tuple(states), unroll=2)
    lax.fori_loop(n_lat // 2, n_lat, functools.partial(body, second_half=True), states, unroll=2)


def _gla(q, k, v, cumf, cumb, gate, kc, vc, cumfc, cumbc, w_norm):
    bn, t, _ = q.shape
    ctx_len = kc.shape[1]
    heads = GLA_HEADS_PER_STEP
    lat = lambda width: pl.BlockSpec((None, t, heads * width), lambda b, h: (b, 0, h))
    ctx = lambda width: pl.BlockSpec((None, ctx_len, heads * width), lambda b, h: (b, 0, h))
    return pl.pallas_call(
        functools.partial(_gla_kernel, heads=heads),
        grid=(bn, G_HEADS // heads),
        in_specs=[lat(G_DK), lat(G_DK), lat(G_DV), lat(G_DK), lat(G_DK), lat(G_DV),
                  ctx(G_DK), ctx(G_DV), ctx(G_DK), ctx(G_DK),
                  pl.BlockSpec((1, G_DV), lambda b, h: (0, 0))],
        out_specs=lat(G_DV),
        out_shape=jax.ShapeDtypeStruct((bn, t, G_V), BF16),
        scratch_shapes=[pltpu.VMEM((t, heads * G_DV), F32), pltpu.VMEM((t, heads * G_DV), F32)],
        compiler_params=_params("arbitrary", "arbitrary"),
        name="gla",
    )(q, k, v, cumf, cumb, gate, kc, vc, cumfc, cumbc, w_norm)


def _odd_out_kernel(o_ref, w_ref, x_ref, gt_ref, out_ref):
    out_ref[...] = x_ref[...] + gt_ref[...] * _dot(o_ref[...], w_ref[...])


def _odd_out(o, w, x, gate, tm):
    bn, t, _ = x.shape
    xs = pl.BlockSpec((None, tm, D_MODEL), lambda j, b: (b, j, 0))
    return pl.pallas_call(
        _odd_out_kernel,
        grid=(t // tm, bn),
        in_specs=[xs, pl.BlockSpec((D_MODEL, D_MODEL), lambda j, b: (0, 0)), xs, _mod_spec(True)],
        out_specs=xs,
        out_shape=jax.ShapeDtypeStruct(x.shape, F32),
        compiler_params=_params("arbitrary", "arbitrary"),
        name="odd_out",
    )(o, w, x, gate)


TOKEN_TILE = 1024
FINAL_TILE = 2048


def _token_tile(t):
    return TOKEN_TILE if t % TOKEN_TILE == 0 else t


def kernel(x, c, ctx, c_ctx, adaln_w, adaln_b, norm_g, w_out, ab_w_in, a_q_norm, a_k_norm, a_sink,
           b_q_norm, b_k_norm, b_lambda_q1, b_lambda_k1, b_lambda_q2, b_lambda_k2, b_subln,
           gla_w_in, gla_wa_f, gla_ba_f, gla_wa_b, gla_ba_b, gla_out_norm):
    bn, t, _ = x.shape
    ctx_len = ctx.shape[1]
    depth = adaln_w.shape[0]
    assert depth == 2

    pad_rows = (-(bn + 1)) % 8
    cond = jnp.concatenate([c, c_ctx[None, :], jnp.zeros((pad_rows, D_MODEL), F32)], axis=0)
    mod = _adaln(cond, adaln_w, adaln_b[:, None, :])

    def mods(layer):
        m = mod[layer]
        shift, scale, gate = (m[:, i * D_MODEL:(i + 1) * D_MODEL] for i in range(3))
        per_x = tuple(v[:bn, None, :] for v in (shift, scale, gate))
        per_c = tuple(v[bn:bn + 1, None, :] for v in (shift, scale, gate))
        return per_x, per_c

    (shx, scx, gtx), (shc, scc, gtc) = mods(0)
    lambda_init = 0.8 - 0.6 * math.exp(-0.3 * 0)
    wt = ab_w_in[0].T.astype(BF16)
    gains = [a_q_norm[0], a_k_norm[0], b_q_norm[0], b_k_norm[0]]
    q_scale = [ATTN_SCALE * LOG2E, 1.0, ATTN_SCALE * LOG2E, 1.0]
    tab_x = _rope_tables(gains, t, True, q_scale)
    tab_c = _rope_tables(gains, ctx_len, False, q_scale)
    g0 = norm_g[0][None, :]
    qa, ka, va, ga, qb, kb, vb, gb = _even_proj(x, g0, scx, shx, wt, tab_x, True, _token_tile(t))
    qac, kac, vac, gac, qbc, kbc, vbc, gbc = _even_proj(ctx, g0, scc, shc, wt, tab_c, False,
                                                        _token_tile(ctx_len))

    sink = a_sink[0].astype(F32) * LOG2E
    oa = _win_attn(qa, ka, va, kac, vac, ga, jnp.repeat(sink, Q_BLOCK)[None, :], True)
    oac = _win_attn(qac, None, None, kac, vac, gac, jnp.repeat(sink, ctx_len)[None, :], False)

    lam = (jnp.exp(jnp.sum(b_lambda_q1[0].astype(F32) * b_lambda_k1[0].astype(F32)))
           - jnp.exp(jnp.sum(b_lambda_q2[0].astype(F32) * b_lambda_k2[0].astype(F32))) + lambda_init)
    tq = 256
    w_sub = jnp.broadcast_to((b_subln[0].astype(F32) * (1.0 - lambda_init))[:, None], (B_VDIM, tq))
    lam_row = jnp.broadcast_to(lam, (1, tq)).astype(F32)
    ob = _diff_attn(qb, kb, vb, kbc, vbc, gb, w_sub, lam_row, tq)
    obc = _ctx_diff_attn(qbc, kbc, vbc, gbc, w_sub, lam_row)

    w0 = w_out[0].astype(BF16)
    (shx1, scx1, gtx1), (shc1, scc1, _) = mods(1)
    w1 = jnp.pad(gla_w_in[0], ((0, 0), (0, ODD_PAD - gla_w_in.shape[2]))).astype(BF16)
    wab = jnp.zeros((128, 2 * G_QK), F32)
    wab = wab.at[0:G_RANK, 0:G_QK].set(gla_wa_f[0])
    wab = wab.at[G_RANK:2 * G_RANK, G_QK:].set(gla_wa_b[0]).astype(BF16)
    bab = jnp.concatenate([gla_ba_f[0], gla_ba_b[0]])[None, :].astype(F32)
    g1 = norm_g[1][None, :]
    x, q, k, v, gate, cumf, cumb = _out_proj(oa, ob, w0, x, gtx, g1, scx1, shx1, w1, wab, bab,
                                             True, _token_tile(t))
    kc, vc, cumfc, cumbc = _out_proj(oac, obc, w0, ctx, gtc, g1, scc1, shc1, w1, wab, bab,
                                     False, _token_tile(ctx_len))
    o = _gla(q, k, v, cumf, cumb, gate, kc, vc, cumfc, cumbc, gla_out_norm[0][None, :].astype(F32))
    return _odd_out(o, w_out[1].astype(BF16), x, gtx1, FINAL_TILE if t % FINAL_TILE == 0 else t)
```

```python
import functools
import math

import jax
import jax.numpy as jnp
from jax import lax
from jax.experimental import pallas as pl
from jax.experimental.pallas import tpu as pltpu

D_MODEL = 1024
GRID_W = 64
HEAD_DIM = 64
ROPE_BASE = 10000.0
EPS = 1e-6
NEG_INF = -1e30
ATTN_SCALE = HEAD_DIM ** -0.5
LOG2E = math.log2(math.e)
WINDOW = 128
Q_BLOCK = 128

A_HEADS = 8
A_KV_HEADS = 2
A_GROUP = A_HEADS // A_KV_HEADS
A_WIDTH = A_HEADS * HEAD_DIM
A_KV_WIDTH = A_KV_HEADS * HEAD_DIM
B_HEADS = 4
B_VDIM = 2 * HEAD_DIM
B_WIDTH = B_HEADS * B_VDIM
EVEN_IN = 2 * A_WIDTH + 2 * A_KV_WIDTH + 4 * B_WIDTH

G_HEADS = 4
G_DK = 128
G_DV = 256
G_RANK = 16
G_TAU = 16.0
G_CHUNK = 64
G_BLOCK = 256
GLA_HEADS_PER_STEP = 2
G_QK = G_HEADS * G_DK
G_V = G_HEADS * G_DV
ODD_MAIN = 2 * G_QK + 2 * G_V
ODD_PAD = ODD_MAIN + 128

VMEM_LIMIT_BYTES = 56 * 1024 * 1024

BF16 = jnp.bfloat16
F32 = jnp.float32


def _params(*semantics):
    return pltpu.CompilerParams(dimension_semantics=semantics,
                                vmem_limit_bytes=VMEM_LIMIT_BYTES)


def _silu(v):
    return v * (1.0 / (1.0 + jnp.exp(-v)))


def _dot(a, b):
    return jnp.dot(a, b, preferred_element_type=F32)


def _dot_nt(a, b):
    return lax.dot_general(a, b, (((1,), (1,)), ((), ())), preferred_element_type=F32)


def _dot_tn(a, b):
    return lax.dot_general(a, b, (((0,), (0,)), ((), ())), preferred_element_type=F32)


def _split_bf16(v):
    hi = v.astype(BF16)
    lo = (v - hi.astype(F32)).astype(BF16)
    return hi, lo


def _adaln_kernel(cond_ref, w_ref, b_ref, o_ref):
    a = _silu(cond_ref[...])
    a_hi, a_lo = _split_bf16(a)
    w_hi, w_lo = _split_bf16(w_ref[...])
    o_ref[...] = (_dot(a_hi, w_hi) + _dot(a_hi, w_lo) + _dot(a_lo, w_hi)) + b_ref[...]


def _adaln(cond, w, b):
    n_layers = w.shape[0]
    rows = cond.shape[0]
    tn = 512
    return pl.pallas_call(
        _adaln_kernel,
        grid=(n_layers, 3 * D_MODEL // tn),
        in_specs=[
            pl.BlockSpec((rows, D_MODEL), lambda l, n: (0, 0)),
            pl.BlockSpec((None, D_MODEL, tn), lambda l, n: (l, 0, n)),
            pl.BlockSpec((None, 1, tn), lambda l, n: (l, 0, n)),
        ],
        out_specs=pl.BlockSpec((None, rows, tn), lambda l, n: (l, 0, n)),
        out_shape=jax.ShapeDtypeStruct((n_layers, rows, 3 * D_MODEL), F32),
        compiler_params=_params("arbitrary", "arbitrary"),
        name="adaln",
    )(cond, w, b)


def _modulate(xf, g_ref, sc_ref, sh_ref):
    r = lax.rsqrt(jnp.mean(xf * xf, axis=-1, keepdims=True) + EPS)
    return ((xf * r) * g_ref[...] * (1.0 + sc_ref[...]) + sh_ref[...]).astype(BF16)


def _modulated(x_ref, g_ref, sc_ref, sh_ref):
    return _modulate(x_ref[...], g_ref, sc_ref, sh_ref)


def _interleave(*gens):
    results = [None] * len(gens)
    live = list(range(len(gens)))
    while live:
        for idx in list(live):
            try:
                next(gens[idx])
            except StopIteration as stop:
                results[idx] = stop.value
                live.remove(idx)
    return results


def _mod_spec(per_batch):
    if per_batch:
        return pl.BlockSpec((None, 1, D_MODEL), lambda j, b: (b, 0, 0))
    return pl.BlockSpec((None, 1, D_MODEL), lambda j, b: (0, 0, 0))


def _rope_head(blk, c, s):
    r = lax.rsqrt(jnp.mean(blk * blk, axis=0, keepdims=True) + EPS)
    partner = jnp.concatenate([blk[16:32], blk[0:16], blk[48:64], blk[32:48]], axis=0)
    return (blk * c + partner * s) * r


def _even_proj_kernel(x_ref, g_ref, sc_ref, sh_ref, wt_ref, tab_ref,
                      qa_ref, ka_ref, va_ref, ga_ref, qb_ref, kb_ref, vb_ref, gb_ref):
    h = _modulated(x_ref, g_ref, sc_ref, sh_ref)

    def rows(start, size):
        return _dot_nt(wt_ref[start:start + size, :], h)

    def normed(acc, table):
        c = tab_ref[2 * table]
        s = tab_ref[2 * table + 1]
        n = acc.shape[0] // HEAD_DIM
        return jnp.concatenate(
            [_rope_head(acc[HEAD_DIM * i:HEAD_DIM * (i + 1)], c, s) for i in range(n)], axis=0)

    off = 0
    qa_ref[...] = normed(rows(off, A_WIDTH), 0).astype(BF16)
    off += A_WIDTH
    kv = rows(off, 2 * A_KV_WIDTH)
    ka_ref[...] = normed(kv[0:A_KV_WIDTH], 1).T.astype(BF16)
    va_ref[...] = kv[A_KV_WIDTH:].astype(BF16)
    off += 2 * A_KV_WIDTH
    ga_ref[...] = rows(off, A_WIDTH).astype(BF16)
    off += A_WIDTH
    qb_ref[...] = normed(rows(off, B_WIDTH), 2).astype(BF16)
    off += B_WIDTH
    kb_ref[...] = normed(rows(off, B_WIDTH), 3).T.astype(BF16)
    off += B_WIDTH
    vb_ref[...] = rows(off, B_WIDTH).astype(BF16)
    off += B_WIDTH
    gb_ref[...] = rows(off, B_WIDTH).astype(BF16)


def _even_proj(x, norm_g, sc, sh, wt, tables, per_batch, tm):
    bn, t, _ = x.shape
    nt = t // tm

    def fm(width):
        return (pl.BlockSpec((None, width, tm), lambda j, b: (b, 0, j)),
                jax.ShapeDtypeStruct((bn, width, t), BF16))

    def tk(width):
        return (pl.BlockSpec((None, tm, width), lambda j, b: (b, j, 0)),
                jax.ShapeDtypeStruct((bn, t, width), BF16))

    outs = [fm(A_WIDTH), tk(A_KV_WIDTH), fm(A_KV_WIDTH), fm(A_WIDTH),
            fm(B_WIDTH), tk(B_WIDTH), fm(B_WIDTH), fm(B_WIDTH)]
    return pl.pallas_call(
        _even_proj_kernel,
        grid=(nt, bn),
        in_specs=[
            pl.BlockSpec((None, tm, D_MODEL), lambda j, b: (b, j, 0)),
            pl.BlockSpec((1, D_MODEL), lambda j, b: (0, 0)),
            _mod_spec(per_batch), _mod_spec(per_batch),
            pl.BlockSpec((EVEN_IN, D_MODEL), lambda j, b: (0, 0)),
            pl.BlockSpec((8, HEAD_DIM, tm), lambda j, b: (0, 0, j)),
        ],
        out_specs=[o[0] for o in outs],
        out_shape=[o[1] for o in outs],
        compiler_params=_params("arbitrary", "arbitrary"),
        name="even_proj",
    )(x, norm_g, sc, sh, wt, tables)


def _rope_tables(gains, t, rotary, q_scale):
    m = HEAD_DIM // 4
    inv = ROPE_BASE ** (-jnp.arange(m, dtype=F32) / m)
    pos = jnp.arange(t, dtype=jnp.int32)
    rows = (pos // GRID_W).astype(F32)
    cols = (pos % GRID_W).astype(F32)
    if rotary:
        ang_r = inv[:, None] * rows[None, :]
        ang_c = inv[:, None] * cols[None, :]
        cos = jnp.concatenate([jnp.cos(ang_r), jnp.cos(ang_r), jnp.cos(ang_c), jnp.cos(ang_c)], axis=0)
        sin = jnp.concatenate([-jnp.sin(ang_r), jnp.sin(ang_r), -jnp.sin(ang_c), jnp.sin(ang_c)], axis=0)
    else:
        cos = jnp.ones((HEAD_DIM, t), F32)
        sin = jnp.zeros((HEAD_DIM, t), F32)
    out = []
    for g, scale in zip(gains, q_scale):
        g = g.astype(F32)
        gp = jnp.concatenate([g[16:32], g[0:16], g[48:64], g[32:48]])
        out.append(g[:, None] * cos * scale)
        out.append(gp[:, None] * sin * scale)
    return jnp.stack(out)


def _win_attn_kernel(*refs, tq, windowed):
    width = A_HEADS * tq
    if windowed:
        (q_ref, qn_ref, kp_ref, kc_ref, kn_ref, kx_ref, vp_ref, vc_ref, vn_ref, vx_ref,
         g_ref, sink_ref, o_ref, qpad_a, qpad_b, s_a, s_b, m_a, m_b) = refs
    else:
        (q_ref, kx_ref, vx_ref, g_ref, sink_ref, o_ref, qpad_a, qpad_b, s_a, s_b, m_a, m_b) = refs
    buf_a = (qpad_a, s_a, m_a)
    buf_b = (qpad_b, s_b, m_b)

    def scores(load_q, key_segs, buf):
        qpad_ref, s_ref, m_ref = buf
        q = load_q()
        qpad_ref[...] = jnp.zeros_like(qpad_ref)
        for hd in range(A_HEADS):
            grp = hd // A_GROUP
            qpad_ref[HEAD_DIM * grp:HEAD_DIM * (grp + 1), tq * hd:tq * (hd + 1)] = (
                q[HEAD_DIM * hd:HEAD_DIM * (hd + 1), :])
        qpad = qpad_ref[...]
        m = sink_ref[...]
        row = 0
        for load_key, kind, valid in key_segs:
            s = _dot(load_key(), qpad)
            n = s.shape[0]
            if kind is not None:
                r = lax.broadcasted_iota(jnp.int32, s.shape, 0)
                cq = lax.broadcasted_iota(jnp.int32, s.shape, 1) % tq
                off = 0 if valid is None else jnp.where(valid, 0, tq)
                ok = (r >= cq + off) if kind == "prev" else (r <= cq - off)
                s = jnp.where(ok, s, NEG_INF)
            s_ref[row:row + n, :] = s
            m = jnp.maximum(m, jnp.max(s, axis=0, keepdims=True))
            row += n
            yield
        m_ref[...] = m

    def attend(value_segs, buf, col0):
        _, s_ref, m_ref = buf
        m = m_ref[...]
        l = jnp.exp2(sink_ref[...] - m)
        acc = jnp.zeros((A_KV_WIDTH, width), F32)
        row = 0
        for load_value in value_segs:
            v = load_value()
            n = v.shape[1]
            p = jnp.exp2(s_ref[row:row + n, :] - m)
            l = l + jnp.sum(p, axis=0, keepdims=True)
            p = p.astype(BF16)
            yield
            acc = acc + _dot(v, p)
            row += n
        inv = 1.0 / l
        for hd in range(A_HEADS):
            grp = hd // A_GROUP
            o = acc[HEAD_DIM * grp:HEAD_DIM * (grp + 1), tq * hd:tq * (hd + 1)]
            o = o * inv[:, tq * hd:tq * (hd + 1)]
            gate = g_ref[HEAD_DIM * hd:HEAD_DIM * (hd + 1), col0:col0 + tq].astype(F32)
            o_ref[HEAD_DIM * hd:HEAD_DIM * (hd + 1), col0:col0 + tq] = (o * _silu(gate)).astype(BF16)

    ctx_keys = (lambda: kx_ref[...], None, None)
    ctx_vals = lambda: vx_ref[...]
    if not windowed:
        _interleave(scores(lambda: q_ref[...], [ctx_keys], buf_a))
        _interleave(attend([ctx_vals], buf_a, 0))
        return

    j = pl.program_id(1)
    last = pl.num_programs(1) - 1
    cur_lo = lambda: kc_ref[0:tq, :]
    cur_hi = lambda: kc_ref[tq:, :]
    nxt_lo = lambda: kn_ref[0:tq, :]
    nxt_hi = lambda: kn_ref[tq:, :]
    x_keys = [(lambda: kp_ref[...], "prev", j > 0), (cur_lo, None, None), (cur_hi, "next", None), ctx_keys]
    y_keys = [(cur_lo, "prev", None), (cur_hi, None, None), (nxt_lo, "next", j < last), ctx_keys]
    z_keys = [(cur_hi, "prev", None), (nxt_lo, None, None), (nxt_hi, "next", None), ctx_keys]
    x_vals = [lambda: vp_ref[...], lambda: vc_ref[:, 0:tq], lambda: vc_ref[:, tq:], ctx_vals]
    y_vals = [lambda: vc_ref[:, 0:tq], lambda: vc_ref[:, tq:], lambda: vn_ref[:, 0:tq], ctx_vals]

    @pl.when(j == 0)
    def _():
        _interleave(scores(lambda: q_ref[:, 0:tq], x_keys, buf_a))

    _interleave(scores(lambda: q_ref[:, tq:], y_keys, buf_b), attend(x_vals, buf_a, 0))
    _interleave(scores(lambda: qn_ref[...], z_keys, buf_a), attend(y_vals, buf_b, tq))


def _win_attn(qa, ka, va, kac, vac, ga, sink_row, windowed):
    bn, _, t = qa.shape
    ctx_len = kac.shape[1]
    kx_spec = pl.BlockSpec((None, ctx_len, A_KV_WIDTH), lambda b, j: (b, 0, 0))
    vx_spec = pl.BlockSpec((None, A_KV_WIDTH, ctx_len), lambda b, j: (b, 0, 0))
    if windowed:
        tq = Q_BLOCK
        nb = t // tq
        assert nb % 2 == 0
        steps = nb // 2
        n_keys = 3 * tq + ctx_len
        before = lambda j: jnp.maximum(2 * j - 1, 0)
        after = lambda j: jnp.minimum(j + 1, steps - 1)
        pair_spec = pl.BlockSpec((None, A_WIDTH, 2 * tq), lambda b, j: (b, 0, j))
        in_specs = [pair_spec,
                    pl.BlockSpec((None, A_WIDTH, tq), lambda b, j: (b, 0, jnp.minimum(2 * j + 2, nb - 1))),
                    pl.BlockSpec((None, tq, A_KV_WIDTH), lambda b, j: (b, before(j), 0)),
                    pl.BlockSpec((None, 2 * tq, A_KV_WIDTH), lambda b, j: (b, j, 0)),
                    pl.BlockSpec((None, 2 * tq, A_KV_WIDTH), lambda b, j: (b, after(j), 0)),
                    kx_spec,
                    pl.BlockSpec((None, A_KV_WIDTH, tq), lambda b, j: (b, 0, before(j))),
                    pl.BlockSpec((None, A_KV_WIDTH, 2 * tq), lambda b, j: (b, 0, j)),
                    pl.BlockSpec((None, A_KV_WIDTH, 2 * tq), lambda b, j: (b, 0, after(j))),
                    vx_spec, pair_spec]
        args = [qa, qa, ka, ka, ka, kac, va, va, va, vac, ga, sink_row]
        out_spec = pair_spec
    else:
        tq = t
        steps = 1
        n_keys = ctx_len
        out_spec = pl.BlockSpec((None, A_WIDTH, tq), lambda b, j: (b, 0, 0))
        in_specs = [out_spec, kx_spec, vx_spec, out_spec]
        args = [qa, kac, vac, ga, sink_row]
    width = A_HEADS * tq
    return pl.pallas_call(
        functools.partial(_win_attn_kernel, tq=tq, windowed=windowed),
        grid=(bn, steps),
        in_specs=in_specs + [pl.BlockSpec((1, width), lambda b, j: (0, 0))],
        out_specs=out_spec,
        out_shape=jax.ShapeDtypeStruct((bn, A_WIDTH, t), BF16),
        scratch_shapes=[pltpu.VMEM((A_KV_WIDTH, width), BF16), pltpu.VMEM((A_KV_WIDTH, width), BF16),
                        pltpu.VMEM((n_keys, width), F32), pltpu.VMEM((n_keys, width), F32),
                        pltpu.VMEM((1, width), F32), pltpu.VMEM((1, width), F32)],
        compiler_params=_params("arbitrary", "arbitrary"),
        name="win_attn" if windowed else "ctx_sink_attn",
    )(*args)


def _key_chunks(segments, key_chunk):
    chunks = []
    base = 0
    for n, load_k, load_v in segments:
        for c0 in range(0, n, key_chunk):
            size = min(key_chunk, n - c0)
            chunks.append((functools.partial(load_k, c0, size), functools.partial(load_v, c0, size),
                           base + c0, size))
        base += n
    return chunks


def _diff_scores(q, chunks, buf, tq):
    qpad_ref, s_ref, m_ref = buf
    qpad_ref[...] = jnp.zeros_like(qpad_ref)
    qpad_ref[0:HEAD_DIM, 0:tq] = q[0:HEAD_DIM, :]
    qpad_ref[HEAD_DIM:, tq:] = q[HEAD_DIM:, :]
    qpad = qpad_ref[...]
    m = None
    for load_k, _, row, size in chunks:
        s = _dot(load_k(), qpad)
        s_ref[row:row + size, :] = s
        s_max = jnp.max(s, axis=0, keepdims=True)
        m = s_max if m is None else jnp.maximum(m, s_max)
        yield
    m_ref[...] = m


def _diff_attend(chunks, buf, tq, lam, emit):
    _, s_ref, m_ref = buf
    m = m_ref[...]
    l = jnp.zeros((1, 2 * tq), F32)
    acc = jnp.zeros((B_VDIM, 2 * tq), F32)
    for _, load_v, row, size in chunks:
        p = jnp.exp2(s_ref[row:row + size, :] - m)
        l = l + jnp.sum(p, axis=0, keepdims=True)
        p = p.astype(BF16)
        yield
        acc = acc + _dot(load_v(), p)
    acc = acc * (1.0 / l)
    emit(acc[:, 0:tq] - lam * acc[:, tq:])


def _diff_emit(g_ref, w_ref, o_ref, rows, cols):
    def emit(o):
        r = lax.rsqrt(jnp.mean(o * o, axis=0, keepdims=True) + EPS)
        o = (o * r) * w_ref[...]
        o_ref[rows, cols] = (o * _silu(g_ref[rows, cols].astype(F32))).astype(BF16)
    return emit


def _diff_attn_kernel(q_ref, qn_ref, k_ref, kx_ref, kz_ref, kxz_ref, v_ref, vx_ref,
                      g_ref, w_ref, lam_ref, o_ref, qpad_a, qpad_b, s_a, s_b, m_a, m_b,
                      *, tq, key_chunk):
    buf_a = (qpad_a, s_a, m_a)
    buf_b = (qpad_b, s_b, m_b)

    def chunks_of(keys, ctx_keys):
        return _key_chunks(
            [(keys.shape[0], lambda c0, n: keys[c0:c0 + n, :], lambda c0, n: v_ref[:, c0:c0 + n]),
             (ctx_keys.shape[0], lambda c0, n: ctx_keys[c0:c0 + n, :], lambda c0, n: vx_ref[:, c0:c0 + n])],
            key_chunk)

    cur = chunks_of(k_ref, kx_ref)
    ahead = chunks_of(kz_ref, kxz_ref)
    lam = lam_ref[...]
    rows = slice(0, B_VDIM)

    @pl.when((pl.program_id(0) == 0) & (pl.program_id(1) == 0) & (pl.program_id(2) == 0))
    def _():
        _interleave(_diff_scores(q_ref[:, 0:tq], cur, buf_a, tq))

    _interleave(_diff_scores(q_ref[:, tq:], cur, buf_b, tq),
                _diff_attend(cur, buf_a, tq, lam, _diff_emit(g_ref, w_ref, o_ref, rows, slice(0, tq))))
    _interleave(_diff_scores(qn_ref[...], ahead, buf_a, tq),
                _diff_attend(cur, buf_b, tq, lam, _diff_emit(g_ref, w_ref, o_ref, rows, slice(tq, 2 * tq))))


def _diff_attn(qb, kb, vb, kbc, vbc, gb, w_sub, lam_row, tq):
    bn, _, t = qb.shape
    t_keys = kb.shape[1]
    ctx_len = kbc.shape[1]
    assert t % (2 * tq) == 0
    nq = t // (2 * tq)

    def following(b, h):
        b_next = jnp.minimum(b + (h + 1) // B_HEADS, bn - 1)
        return b_next, jnp.where(b + (h + 1) // B_HEADS > bn - 1, h, (h + 1) % B_HEADS)

    def next_tile(b, h, j):
        b_next, h_next = following(b, h)
        wrap = j == nq - 1
        return jnp.where(wrap, b_next, b), jnp.where(wrap, h_next, h), jnp.where(wrap, 0, 2 * (j + 1))

    def keys_ahead(b, h, j):
        b_next, h_next, _ = next_tile(b, h, j)
        return b_next, 0, h_next

    head_fm = lambda width: pl.BlockSpec((None, B_VDIM, width), lambda b, h, j: (b, h, 0))
    pair_spec = pl.BlockSpec((None, B_VDIM, 2 * tq), lambda b, h, j: (b, h, j))
    in_specs = [
        pair_spec,
        pl.BlockSpec((None, B_VDIM, tq), lambda b, h, j: next_tile(b, h, j)),
        pl.BlockSpec((None, t_keys, B_VDIM), lambda b, h, j: (b, 0, h)),
        pl.BlockSpec((None, ctx_len, B_VDIM), lambda b, h, j: (b, 0, h)),
        pl.BlockSpec((None, t_keys, B_VDIM), lambda b, h, j: keys_ahead(b, h, j)),
        pl.BlockSpec((None, ctx_len, B_VDIM), lambda b, h, j: keys_ahead(b, h, j)),
        head_fm(t_keys), head_fm(ctx_len), pair_spec,
        pl.BlockSpec((B_VDIM, tq), lambda b, h, j: (0, 0)),
        pl.BlockSpec((1, tq), lambda b, h, j: (0, 0)),
    ]
    n_keys = t_keys + ctx_len
    return pl.pallas_call(
        functools.partial(_diff_attn_kernel, tq=tq, key_chunk=512),
        grid=(bn, B_HEADS, nq),
        in_specs=in_specs,
        out_specs=pair_spec,
        out_shape=jax.ShapeDtypeStruct((bn, B_WIDTH, t), BF16),
        scratch_shapes=[pltpu.VMEM((B_VDIM, 2 * tq), BF16), pltpu.VMEM((B_VDIM, 2 * tq), BF16),
                        pltpu.VMEM((n_keys, 2 * tq), F32), pltpu.VMEM((n_keys, 2 * tq), F32),
                        pltpu.VMEM((1, 2 * tq), F32), pltpu.VMEM((1, 2 * tq), F32)],
        compiler_params=_params("arbitrary", "arbitrary", "arbitrary"),
        name="diff_attn",
    )(qb, qb, kb, kbc, kb, kbc, vb, vbc, gb, w_sub, lam_row)


def _ctx_diff_attn_kernel(q_ref, kx_ref, vx_ref, g_ref, w_ref, lam_ref, o_ref,
                          qpad_a, qpad_b, s_a, s_b, m_a, m_b, *, tq, key_chunk):
    bufs = [(qpad_a, s_a, m_a), (qpad_b, s_b, m_b)]
    lam = lam_ref[...]
    n = kx_ref.shape[0]

    def head(h):
        rows = slice(B_VDIM * h, B_VDIM * (h + 1))
        chunks = _key_chunks([(n, lambda c0, size: kx_ref[c0:c0 + size, rows],
                               lambda c0, size: vx_ref[rows, c0:c0 + size])], key_chunk)
        return rows, chunks

    rows, chunks = head(0)
    _interleave(_diff_scores(q_ref[rows, :], chunks, bufs[0], tq))
    for h in range(B_HEADS):
        rows, chunks = head(h)
        gens = [_diff_attend(chunks, bufs[h % 2], tq, lam,
                             _diff_emit(g_ref, w_ref, o_ref, rows, slice(0, tq)))]
        if h + 1 < B_HEADS:
            rows_n, chunks_n = head(h + 1)
            gens.insert(0, _diff_scores(q_ref[rows_n, :], chunks_n, bufs[(h + 1) % 2], tq))
        _interleave(*gens)


def _ctx_diff_attn(qbc, kbc, vbc, gbc, w_sub, lam_row):
    bn, _, tq = qbc.shape
    n_keys = kbc.shape[1]
    fm = pl.BlockSpec((None, B_WIDTH, tq), lambda b: (b, 0, 0))
    return pl.pallas_call(
        functools.partial(_ctx_diff_attn_kernel, tq=tq, key_chunk=512),
        grid=(bn,),
        in_specs=[fm, pl.BlockSpec((None, n_keys, B_WIDTH), lambda b: (b, 0, 0)),
                  pl.BlockSpec((None, B_WIDTH, n_keys), lambda b: (b, 0, 0)), fm,
                  pl.BlockSpec((B_VDIM, tq), lambda b: (0, 0)),
                  pl.BlockSpec((1, tq), lambda b: (0, 0))],
        out_specs=fm,
        out_shape=jax.ShapeDtypeStruct((bn, B_WIDTH, tq), BF16),
        scratch_shapes=[pltpu.VMEM((B_VDIM, 2 * tq), BF16), pltpu.VMEM((B_VDIM, 2 * tq), BF16),
                        pltpu.VMEM((n_keys, 2 * tq), F32), pltpu.VMEM((n_keys, 2 * tq), F32),
                        pltpu.VMEM((1, 2 * tq), F32), pltpu.VMEM((1, 2 * tq), F32)],
        compiler_params=_params("arbitrary"),
        name="ctx_diff_attn",
    )(qbc, kbc, vbc, gbc, w_sub, lam_row)


def _out_proj_kernel(*refs, latent):
    (oa_ref, ob_ref, w0_ref, x_ref, gt_ref, g_ref, sc_ref, sh_ref, w_ref, wab_ref, bab_ref) = refs[:11]
    if latent:
        xo_ref, q_ref, k_ref, v_ref, gate_ref, cumf_ref, cumb_ref = refs[11:]
    else:
        k_ref, v_ref, cumf_ref, cumb_ref = refs[11:]
    y = _dot_tn(oa_ref[...], w0_ref[0:A_WIDTH, :]) + _dot_tn(ob_ref[...], w0_ref[A_WIDTH:, :])
    x_new = x_ref[...] + gt_ref[...] * y
    h = _modulate(x_new, g_ref, sc_ref, sh_ref)

    def cols(start, size):
        return _dot(h, w_ref[:, start:start + size])

    low = cols(ODD_MAIN, 128).astype(BF16)
    k_ref[...] = cols(G_QK, G_QK).astype(BF16)
    z = _dot(low, wab_ref[...]) + bab_ref[...]
    la = (jnp.minimum(z, 0.0) - jnp.log(1.0 + jnp.exp(-jnp.abs(z)))) / G_TAU
    for i in range(G_V // 512):
        v_ref[:, 512 * i:512 * (i + 1)] = cols(2 * G_QK + 512 * i, 512).astype(BF16)
    if latent:
        xo_ref[...] = x_new
        q_ref[...] = (cols(0, G_QK) * (G_DK ** -0.5)).astype(BF16)
        for i in range(G_V // 512):
            gate_ref[:, 512 * i:512 * (i + 1)] = _silu(cols(2 * G_QK + G_V + 512 * i, 512)).astype(BF16)
    r = lax.broadcasted_iota(jnp.int32, (G_BLOCK, G_BLOCK), 0)
    c = lax.broadcasted_iota(jnp.int32, (G_BLOCK, G_BLOCK), 1)
    in_chunk = r % G_CHUNK
    tri_f = jnp.where((r - c).astype(jnp.uint32) <= in_chunk.astype(jnp.uint32), 1.0, 0.0)
    tri_b = jnp.where((c - r).astype(jnp.uint32) <= (G_CHUNK - 1 - in_chunk).astype(jnp.uint32),
                      1.0, 0.0)
    for blk in range(la.shape[0] // G_BLOCK):
        rows = slice(blk * G_BLOCK, (blk + 1) * G_BLOCK)
        for tri, lo_col, out_ref in ((tri_f, 0, cumf_ref), (tri_b, G_QK, cumb_ref)):
            hi, lo = _split_bf16(la[rows, lo_col:lo_col + G_QK])
            cum = _dot(tri.astype(BF16), jnp.concatenate([hi, lo], axis=1))
            out_ref[rows, :] = cum[:, :G_QK] + cum[:, G_QK:]


def _out_proj(oa, ob, w0, x, gate, norm_g, sc, sh, w, wab, bab, latent, tm):
    bn, t, _ = x.shape
    fm = pl.BlockSpec((None, A_WIDTH, tm), lambda j, b: (b, 0, j))

    def tk(width, dtype):
        return (pl.BlockSpec((None, tm, width), lambda j, b: (b, j, 0)),
                jax.ShapeDtypeStruct((bn, t, width), dtype))

    const = lambda rows, cols: pl.BlockSpec((rows, cols), lambda j, b: (0, 0))
    x_spec = tk(D_MODEL, F32)
    tail = [tk(G_QK, BF16), tk(G_V, BF16), tk(G_QK, F32), tk(G_QK, F32)]
    if latent:
        outs = [x_spec, tk(G_QK, BF16), tail[0], tail[1], tk(G_V, BF16), tail[2], tail[3]]
    else:
        outs = tail
    return pl.pallas_call(
        functools.partial(_out_proj_kernel, latent=latent),
        grid=(t // tm, bn),
        in_specs=[
            fm, fm, const(D_MODEL, D_MODEL), x_spec[0], _mod_spec(latent),
            const(1, D_MODEL), _mod_spec(latent), _mod_spec(latent),
            const(D_MODEL, ODD_PAD), const(128, 2 * G_QK), const(1, 2 * G_QK),
        ],
        out_specs=[o[0] for o in outs],
        out_shape=[o[1] for o in outs],
        compiler_params=_params("arbitrary", "arbitrary"),
        name="out_proj" if latent else "ctx_out_proj",
    )(oa, ob, w0, x, gate, norm_g, sc, sh, w, wab, bab)


def _gla_block(q, k, v, b, state, reverse, with_out):
    n = G_BLOCK // G_CHUNK
    ends = [(j * G_CHUNK if reverse else (j + 1) * G_CHUNK - 1) for j in range(n)]
    b_last = jnp.concatenate(
        [jnp.broadcast_to(b[e:e + 1, :], (G_CHUNK, G_DK)) for e in ends], axis=0)
    kf = k.astype(F32)
    k_state = (kf * jnp.exp(b_last - b)).astype(BF16)
    if with_out:
        qd = (q.astype(F32) * jnp.exp(b)).astype(BF16)
        kd = (kf * jnp.exp(-b)).astype(BF16)
    yield
    if with_out:
        att = _dot_nt(qd, kd)
        yield
    chunks = [slice(j * G_CHUNK, (j + 1) * G_CHUNK) for j in range(n)]
    d_state = [_dot_tn(k_state[sl], v[sl]) for sl in chunks]
    decay_t = jnp.exp(b_last).T
    yield
    if with_out:
        r = lax.broadcasted_iota(jnp.int32, (G_BLOCK, G_BLOCK), 0)
        c = lax.broadcasted_iota(jnp.int32, (G_BLOCK, G_BLOCK), 1)
        in_chunk = r % G_CHUNK
        if reverse:
            causal = (c - r).astype(jnp.uint32) <= (G_CHUNK - 1 - in_chunk).astype(jnp.uint32)
        else:
            causal = (r - c).astype(jnp.uint32) <= in_chunk.astype(jnp.uint32)
        intra = _dot(jnp.where(causal, att, 0.0).astype(BF16), v)
        yield
    outs = [None] * n
    for j in (reversed(range(n)) if reverse else range(n)):
        if with_out:
            outs[j] = intra[chunks[j]] + _dot(qd[chunks[j]], state.astype(BF16))
        decay = jnp.broadcast_to(decay_t[:, ends[j]:ends[j] + 1], (G_DK, G_DV))
        state = state * decay + d_state[j]
        yield
    return (jnp.concatenate(outs, axis=0) if with_out else None), state


def _gla_kernel(q_ref, k_ref, v_ref, cumf_ref, cumb_ref, g_ref,
                kc_ref, vc_ref, cumfc_ref, cumbc_ref, w_ref, o_ref, of_ref, ob_ref, *, heads):
    n_lat = q_ref.shape[0] // G_BLOCK
    n_ctx = kc_ref.shape[0] // G_BLOCK
    qk = [slice(G_DK * h, G_DK * (h + 1)) for h in range(heads)]
    vv = [slice(G_DV * h, G_DV * (h + 1)) for h in range(heads)]

    def rows(i):
        return pl.ds(pl.multiple_of(i * G_BLOCK, G_BLOCK), G_BLOCK)

    states = [jnp.zeros((G_DK, G_DV), F32)] * (2 * heads)
    for i in range(n_ctx):
        sl_f = slice(i * G_BLOCK, (i + 1) * G_BLOCK)
        sl_b = slice((n_ctx - 1 - i) * G_BLOCK, (n_ctx - i) * G_BLOCK)
        gens = []
        for h in range(heads):
            gens.append(_gla_block(None, kc_ref[sl_f, qk[h]], vc_ref[sl_f, vv[h]],
                                   cumfc_ref[sl_f, qk[h]], states[2 * h], False, False))
            gens.append(_gla_block(None, kc_ref[sl_b, qk[h]], vc_ref[sl_b, vv[h]],
                                   cumbc_ref[sl_b, qk[h]], states[2 * h + 1], True, False))
        states = [st for _, st in _interleave(*gens)]

    def finish(i, h, o, other_ref):
        o = o + other_ref[rows(i), vv[h]]
        rn = lax.rsqrt(jnp.mean(o * o, axis=-1, keepdims=True) + EPS)
        o = (o * rn) * w_ref[...]
        o_ref[rows(i), vv[h]] = (o * g_ref[rows(i), vv[h]].astype(F32)).astype(BF16)

    def body(i, states, second_half):
        i_b = n_lat - 1 - i
        gens = []
        for h in range(heads):
            gens.append(_gla_block(q_ref[rows(i), qk[h]], k_ref[rows(i), qk[h]], v_ref[rows(i), vv[h]],
                                   cumf_ref[rows(i), qk[h]], states[2 * h], False, True))
            gens.append(_gla_block(q_ref[rows(i_b), qk[h]], k_ref[rows(i_b), qk[h]],
                                   v_ref[rows(i_b), vv[h]], cumb_ref[rows(i_b), qk[h]],
                                   states[2 * h + 1], True, True))
        results = _interleave(*gens)
        for h in range(heads):
            (o_f, _), (o_b, _) = results[2 * h], results[2 * h + 1]
            if second_half:
                finish(i, h, o_f, ob_ref)
                finish(i_b, h, o_b, of_ref)
            else:
                of_ref[rows(i), vv[h]] = o_f
                ob_ref[rows(i_b), vv[h]] = o_b
        return tuple(st for _, st in results)

    assert n_lat % 2 == 0
    states = lax.fori_loop(0, n_lat // 2, functools.partial(body, second_half=False),
                           tuple(states), unroll=2)
    lax.fori_loop(n_lat // 2, n_lat, functools.partial(body, second_half=True), states, unroll=2)


def _gla(q, k, v, cumf, cumb, gate, kc, vc, cumfc, cumbc, w_norm):
    bn, t, _ = q.shape
    ctx_len = kc.shape[1]
    heads = GLA_HEADS_PER_STEP
    lat = lambda width: pl.BlockSpec((None, t, heads * width), lambda b, h: (b, 0, h))
    ctx = lambda width: pl.BlockSpec((None, ctx_len, heads * width), lambda b, h: (b, 0, h))
    return pl.pallas_call(
        functools.partial(_gla_kernel, heads=heads),
        grid=(bn, G_HEADS // heads),
        in_specs=[lat(G_DK), lat(G_DK), lat(G_DV), lat(G_DK), lat(G_DK), lat(G_DV),
                  ctx(G_DK), ctx(G_DV), ctx(G_DK), ctx(G_DK),
                  pl.BlockSpec((1, G_DV), lambda b, h: (0, 0))],
        out_specs=lat(G_DV),
        out_shape=jax.ShapeDtypeStruct((bn, t, G_V), BF16),
        scratch_shapes=[pltpu.VMEM((t, heads * G_DV), F32), pltpu.VMEM((t, heads * G_DV), F32)],
        compiler_params=_params("arbitrary", "arbitrary"),
        name="gla",
    )(q, k, v, cumf, cumb, gate, kc, vc, cumfc, cumbc, w_norm)


def _odd_out_kernel(o_ref, w_ref, x_ref, gt_ref, out_ref):
    out_ref[...] = x_ref[...] + gt_ref[...] * _dot(o_ref[...], w_ref[...])


def _odd_out(o, w, x, gate, tm):
    bn, t, _ = x.shape
    xs = pl.BlockSpec((None, tm, D_MODEL), lambda j, b: (b, j, 0))
    return pl.pallas_call(
        _odd_out_kernel,
        grid=(t // tm, bn),
        in_specs=[xs, pl.BlockSpec((D_MODEL, D_MODEL), lambda j, b: (0, 0)), xs, _mod_spec(True)],
        out_specs=xs,
        out_shape=jax.ShapeDtypeStruct(x.shape, F32),
        compiler_params=_params("arbitrary", "arbitrary"),
        name="odd_out",
    )(o, w, x, gate)


TOKEN_TILE = 1024
OUT_PROJ_TILE = 512
FINAL_TILE = 2048


def _token_tile(t):
    return TOKEN_TILE if t % TOKEN_TILE == 0 else t


def kernel(x, c, ctx, c_ctx, adaln_w, adaln_b, norm_g, w_out, ab_w_in, a_q_norm, a_k_norm, a_sink,
           b_q_norm, b_k_norm, b_lambda_q1, b_lambda_k1, b_lambda_q2, b_lambda_k2, b_subln,
           gla_w_in, gla_wa_f, gla_ba_f, gla_wa_b, gla_ba_b, gla_out_norm):
    bn, t, _ = x.shape
    ctx_len = ctx.shape[1]
    depth = adaln_w.shape[0]
    assert depth == 2

    pad_rows = (-(bn + 1)) % 8
    cond = jnp.concatenate([c, c_ctx[None, :], jnp.zeros((pad_rows, D_MODEL), F32)], axis=0)
    mod = _adaln(cond, adaln_w, adaln_b[:, None, :])

    def mods(layer):
        m = mod[layer]
        shift, scale, gate = (m[:, i * D_MODEL:(i + 1) * D_MODEL] for i in range(3))
        per_x = tuple(v[:bn, None, :] for v in (shift, scale, gate))
        per_c = tuple(v[bn:bn + 1, None, :] for v in (shift, scale, gate))
        return per_x, per_c

    (shx, scx, gtx), (shc, scc, gtc) = mods(0)
    lambda_init = 0.8 - 0.6 * math.exp(-0.3 * 0)
    wt = ab_w_in[0].T.astype(BF16)
    gains = [a_q_norm[0], a_k_norm[0], b_q_norm[0], b_k_norm[0]]
    q_scale = [ATTN_SCALE * LOG2E, 1.0, ATTN_SCALE * LOG2E, 1.0]
    tab_x = _rope_tables(gains, t, True, q_scale)
    tab_c = _rope_tables(gains, ctx_len, False, q_scale)
    g0 = norm_g[0][None, :]
    qa, ka, va, ga, qb, kb, vb, gb = _even_proj(x, g0, scx, shx, wt, tab_x, True, _token_tile(t))
    qac, kac, vac, gac, qbc, kbc, vbc, gbc = _even_proj(ctx, g0, scc, shc, wt, tab_c, False,
                                                        _token_tile(ctx_len))

    sink = a_sink[0].astype(F32) * LOG2E
    oa = _win_attn(qa, ka, va, kac, vac, ga, jnp.repeat(sink, Q_BLOCK)[None, :], True)
    oac = _win_attn(qac, None, None, kac, vac, gac, jnp.repeat(sink, ctx_len)[None, :], False)

    lam = (jnp.exp(jnp.sum(b_lambda_q1[0].astype(F32) * b_lambda_k1[0].astype(F32)))
           - jnp.exp(jnp.sum(b_lambda_q2[0].astype(F32) * b_lambda_k2[0].astype(F32))) + lambda_init)
    tq = 256
    w_sub = jnp.broadcast_to((b_subln[0].astype(F32) * (1.0 - lambda_init))[:, None], (B_VDIM, tq))
    lam_row = jnp.broadcast_to(lam, (1, tq)).astype(F32)
    ob = _diff_attn(qb, kb, vb, kbc, vbc, gb, w_sub, lam_row, tq)
    obc = _ctx_diff_attn(qbc, kbc, vbc, gbc, w_sub, lam_row)

    w0 = w_out[0].astype(BF16)
    (shx1, scx1, gtx1), (shc1, scc1, _) = mods(1)
    w1 = jnp.pad(gla_w_in[0], ((0, 0), (0, ODD_PAD - gla_w_in.shape[2]))).astype(BF16)
    wab = jnp.zeros((128, 2 * G_QK), F32)
    wab = wab.at[0:G_RANK, 0:G_QK].set(gla_wa_f[0])
    wab = wab.at[G_RANK:2 * G_RANK, G_QK:].set(gla_wa_b[0]).astype(BF16)
    bab = jnp.concatenate([gla_ba_f[0], gla_ba_b[0]])[None, :].astype(F32)
    g1 = norm_g[1][None, :]
    x, q, k, v, gate, cumf, cumb = _out_proj(oa, ob, w0, x, gtx, g1, scx1, shx1, w1, wab, bab,
                                             True, OUT_PROJ_TILE if t % OUT_PROJ_TILE == 0 else t)
    kc, vc, cumfc, cumbc = _out_proj(oac, obc, w0, ctx, gtc, g1, scc1, shc1, w1, wab, bab,
                                     False, _token_tile(ctx_len))
    o = _gla(q, k, v, cumf, cumb, gate, kc, vc, cumfc, cumbc, gla_out_norm[0][None, :].astype(F32))
    return _odd_out(o, w_out[1].astype(BF16), x, gtx1, FINAL_TILE if t % FINAL_TILE == 0 else t)
```

```python
import functools
import math

import jax
import jax.numpy as jnp
from jax import lax
from jax.experimental import pallas as pl
from jax.experimental.pallas import tpu as pltpu

D_MODEL = 1024
GRID_W = 64
HEAD_DIM = 64
ROPE_BASE = 10000.0
EPS = 1e-6
NEG_INF = -1e30
ATTN_SCALE = HEAD_DIM ** -0.5
LOG2E = math.log2(math.e)
WINDOW = 128
Q_BLOCK = 128

A_HEADS = 8
A_KV_HEADS = 2
A_GROUP = A_HEADS // A_KV_HEADS
A_WIDTH = A_HEADS * HEAD_DIM
A_KV_WIDTH = A_KV_HEADS * HEAD_DIM
B_HEADS = 4
B_VDIM = 2 * HEAD_DIM
B_WIDTH = B_HEADS * B_VDIM
EVEN_IN = 2 * A_WIDTH + 2 * A_KV_WIDTH + 4 * B_WIDTH

G_HEADS = 4
G_DK = 128
G_DV = 256
G_RANK = 16
G_TAU = 16.0
G_CHUNK = 64
G_BLOCK = 256
GLA_HEADS_PER_STEP = 2
WIN_BLOCKS = 8
DIFF_TILES = 4
G_QK = G_HEADS * G_DK
G_V = G_HEADS * G_DV
ODD_MAIN = 2 * G_QK + 2 * G_V
ODD_PAD = ODD_MAIN + 128

VMEM_LIMIT_BYTES = 56 * 1024 * 1024

BF16 = jnp.bfloat16
F32 = jnp.float32


def _params(*semantics):
    return pltpu.CompilerParams(dimension_semantics=semantics,
                                vmem_limit_bytes=VMEM_LIMIT_BYTES)


def _silu(v):
    return v * (1.0 / (1.0 + jnp.exp(-v)))


def _dot(a, b):
    return jnp.dot(a, b, preferred_element_type=F32)


def _dot_nt(a, b):
    return lax.dot_general(a, b, (((1,), (1,)), ((), ())), preferred_element_type=F32)


def _dot_tn(a, b):
    return lax.dot_general(a, b, (((0,), (0,)), ((), ())), preferred_element_type=F32)


def _split_bf16(v):
    hi = v.astype(BF16)
    lo = (v - hi.astype(F32)).astype(BF16)
    return hi, lo


def _adaln_kernel(cond_ref, w_ref, b_ref, o_ref):
    a = _silu(cond_ref[...])
    a_hi, a_lo = _split_bf16(a)
    w_hi, w_lo = _split_bf16(w_ref[...])
    o_ref[...] = (_dot(a_hi, w_hi) + _dot(a_hi, w_lo) + _dot(a_lo, w_hi)) + b_ref[...]


def _adaln(cond, w, b):
    n_layers = w.shape[0]
    rows = cond.shape[0]
    tn = 512
    return pl.pallas_call(
        _adaln_kernel,
        grid=(n_layers, 3 * D_MODEL // tn),
        in_specs=[
            pl.BlockSpec((rows, D_MODEL), lambda l, n: (0, 0)),
            pl.BlockSpec((None, D_MODEL, tn), lambda l, n: (l, 0, n)),
            pl.BlockSpec((None, 1, tn), lambda l, n: (l, 0, n)),
        ],
        out_specs=pl.BlockSpec((None, rows, tn), lambda l, n: (l, 0, n)),
        out_shape=jax.ShapeDtypeStruct((n_layers, rows, 3 * D_MODEL), F32),
        compiler_params=_params("arbitrary", "arbitrary"),
        name="adaln",
    )(cond, w, b)


def _modulate(xf, g_ref, sc_ref, sh_ref):
    r = lax.rsqrt(jnp.mean(xf * xf, axis=-1, keepdims=True) + EPS)
    return ((xf * r) * g_ref[...] * (1.0 + sc_ref[...]) + sh_ref[...]).astype(BF16)


def _modulated(x_ref, g_ref, sc_ref, sh_ref):
    return _modulate(x_ref[...], g_ref, sc_ref, sh_ref)


def _interleave(*gens):
    results = [None] * len(gens)
    live = list(range(len(gens)))
    while live:
        for idx in list(live):
            try:
                next(gens[idx])
            except StopIteration as stop:
                results[idx] = stop.value
                live.remove(idx)
    return results


def _mod_spec(per_batch):
    if per_batch:
        return pl.BlockSpec((None, 1, D_MODEL), lambda j, b: (b, 0, 0))
    return pl.BlockSpec((None, 1, D_MODEL), lambda j, b: (0, 0, 0))


def _rope_head(blk, c, s):
    r = lax.rsqrt(jnp.mean(blk * blk, axis=0, keepdims=True) + EPS)
    partner = jnp.concatenate([blk[16:32], blk[0:16], blk[48:64], blk[32:48]], axis=0)
    return (blk * c + partner * s) * r


def _even_proj_kernel(x_ref, g_ref, sc_ref, sh_ref, wt_ref, tab_ref,
                      qa_ref, ka_ref, va_ref, ga_ref, qb_ref, kb_ref, vb_ref, gb_ref):
    h = _modulated(x_ref, g_ref, sc_ref, sh_ref)

    def rows(start, size):
        return _dot_nt(wt_ref[start:start + size, :], h)

    def normed(acc, table):
        c = tab_ref[2 * table]
        s = tab_ref[2 * table + 1]
        n = acc.shape[0] // HEAD_DIM
        return jnp.concatenate(
            [_rope_head(acc[HEAD_DIM * i:HEAD_DIM * (i + 1)], c, s) for i in range(n)], axis=0)

    off = 0
    qa_ref[...] = normed(rows(off, A_WIDTH), 0).astype(BF16)
    off += A_WIDTH
    kv = rows(off, 2 * A_KV_WIDTH)
    ka_ref[...] = normed(kv[0:A_KV_WIDTH], 1).T.astype(BF16)
    va_ref[...] = kv[A_KV_WIDTH:].astype(BF16)
    off += 2 * A_KV_WIDTH
    ga_ref[...] = rows(off, A_WIDTH).astype(BF16)
    off += A_WIDTH
    qb_ref[...] = normed(rows(off, B_WIDTH), 2).astype(BF16)
    off += B_WIDTH
    kb_ref[...] = normed(rows(off, B_WIDTH), 3).T.astype(BF16)
    off += B_WIDTH
    vb_ref[...] = rows(off, B_WIDTH).astype(BF16)
    off += B_WIDTH
    gb_ref[...] = rows(off, B_WIDTH).astype(BF16)


def _even_proj(x, norm_g, sc, sh, wt, tables, per_batch, tm):
    bn, t, _ = x.shape
    nt = t // tm

    def fm(width):
        return (pl.BlockSpec((None, width, tm), lambda j, b: (b, 0, j)),
                jax.ShapeDtypeStruct((bn, width, t), BF16))

    def tk(width):
        return (pl.BlockSpec((None, tm, width), lambda j, b: (b, j, 0)),
                jax.ShapeDtypeStruct((bn, t, width), BF16))

    outs = [fm(A_WIDTH), tk(A_KV_WIDTH), fm(A_KV_WIDTH), fm(A_WIDTH),
            fm(B_WIDTH), tk(B_WIDTH), fm(B_WIDTH), fm(B_WIDTH)]
    return pl.pallas_call(
        _even_proj_kernel,
        grid=(nt, bn),
        in_specs=[
            pl.BlockSpec((None, tm, D_MODEL), lambda j, b: (b, j, 0)),
            pl.BlockSpec((1, D_MODEL), lambda j, b: (0, 0)),
            _mod_spec(per_batch), _mod_spec(per_batch),
            pl.BlockSpec((EVEN_IN, D_MODEL), lambda j, b: (0, 0)),
            pl.BlockSpec((8, HEAD_DIM, tm), lambda j, b: (0, 0, j)),
        ],
        out_specs=[o[0] for o in outs],
        out_shape=[o[1] for o in outs],
        compiler_params=_params("arbitrary", "arbitrary"),
        name="even_proj",
    )(x, norm_g, sc, sh, wt, tables)


def _rope_tables(gains, t, rotary, q_scale):
    m = HEAD_DIM // 4
    inv = ROPE_BASE ** (-jnp.arange(m, dtype=F32) / m)
    pos = jnp.arange(t, dtype=jnp.int32)
    rows = (pos // GRID_W).astype(F32)
    cols = (pos % GRID_W).astype(F32)
    if rotary:
        ang_r = inv[:, None] * rows[None, :]
        ang_c = inv[:, None] * cols[None, :]
        cos = jnp.concatenate([jnp.cos(ang_r), jnp.cos(ang_r), jnp.cos(ang_c), jnp.cos(ang_c)], axis=0)
        sin = jnp.concatenate([-jnp.sin(ang_r), jnp.sin(ang_r), -jnp.sin(ang_c), jnp.sin(ang_c)], axis=0)
    else:
        cos = jnp.ones((HEAD_DIM, t), F32)
        sin = jnp.zeros((HEAD_DIM, t), F32)
    out = []
    for g, scale in zip(gains, q_scale):
        g = g.astype(F32)
        gp = jnp.concatenate([g[16:32], g[0:16], g[48:64], g[32:48]])
        out.append(g[:, None] * cos * scale)
        out.append(gp[:, None] * sin * scale)
    return jnp.stack(out)


def _win_attn_kernel(*refs, tq, windowed, blocks):
    width = A_HEADS * tq
    if windowed:
        (q_ref, qn_ref, kp_ref, kc_ref, kn_ref, kx_ref, vp_ref, vc_ref, vn_ref, vx_ref,
         g_ref, sink_ref, o_ref, qpad_a, qpad_b, s_a, s_b, m_a, m_b) = refs
    else:
        (q_ref, kx_ref, vx_ref, g_ref, sink_ref, o_ref, qpad_a, qpad_b, s_a, s_b, m_a, m_b) = refs
    buf_a = (qpad_a, s_a, m_a)
    buf_b = (qpad_b, s_b, m_b)

    def scores(load_q, key_segs, buf):
        qpad_ref, s_ref, m_ref = buf
        q = load_q()
        qpad_ref[...] = jnp.zeros_like(qpad_ref)
        for hd in range(A_HEADS):
            grp = hd // A_GROUP
            qpad_ref[HEAD_DIM * grp:HEAD_DIM * (grp + 1), tq * hd:tq * (hd + 1)] = (
                q[HEAD_DIM * hd:HEAD_DIM * (hd + 1), :])
        qpad = qpad_ref[...]
        m = sink_ref[...]
        row = 0
        for load_key, kind, valid in key_segs:
            s = _dot(load_key(), qpad)
            n = s.shape[0]
            if kind is not None:
                r = lax.broadcasted_iota(jnp.int32, s.shape, 0)
                cq = lax.broadcasted_iota(jnp.int32, s.shape, 1) % tq
                off = 0 if valid is None else jnp.where(valid, 0, tq)
                ok = (r >= cq + off) if kind == "prev" else (r <= cq - off)
                s = jnp.where(ok, s, NEG_INF)
            s_ref[row:row + n, :] = s
            m = jnp.maximum(m, jnp.max(s, axis=0, keepdims=True))
            row += n
            yield
        m_ref[...] = m

    def attend(value_segs, buf, col0):
        _, s_ref, m_ref = buf
        m = m_ref[...]
        l = jnp.exp2(sink_ref[...] - m)
        acc = jnp.zeros((A_KV_WIDTH, width), F32)
        row = 0
        for load_value in value_segs:
            v = load_value()
            n = v.shape[1]
            p = jnp.exp2(s_ref[row:row + n, :] - m)
            l = l + jnp.sum(p, axis=0, keepdims=True)
            p = p.astype(BF16)
            yield
            acc = acc + _dot(v, p)
            row += n
        inv = 1.0 / l
        for hd in range(A_HEADS):
            grp = hd // A_GROUP
            o = acc[HEAD_DIM * grp:HEAD_DIM * (grp + 1), tq * hd:tq * (hd + 1)]
            o = o * inv[:, tq * hd:tq * (hd + 1)]
            gate = g_ref[HEAD_DIM * hd:HEAD_DIM * (hd + 1), col0:col0 + tq].astype(F32)
            o_ref[HEAD_DIM * hd:HEAD_DIM * (hd + 1), col0:col0 + tq] = (o * _silu(gate)).astype(BF16)

    ctx_keys = (lambda: kx_ref[...], None, None)
    ctx_vals = lambda: vx_ref[...]
    if not windowed:
        _interleave(scores(lambda: q_ref[...], [ctx_keys], buf_a))
        _interleave(attend([ctx_vals], buf_a, 0))
        return

    j = pl.program_id(1)
    last = pl.num_programs(1) - 1
    n = blocks

    def cur(i):
        return lambda: kc_ref[i * tq:(i + 1) * tq, :]

    def cur_v(i):
        return lambda: vc_ref[:, i * tq:(i + 1) * tq]

    def keys_of(i):
        if i == 0:
            prev = (lambda: kp_ref[...], "prev", j > 0)
        else:
            prev = (cur(i - 1), "prev", None)
        if i < n:
            own = (cur(i), None, None)
        else:
            own = (lambda: kn_ref[0:tq, :], None, None)
        if i < n - 1:
            nxt = (cur(i + 1), "next", None)
        elif i == n - 1:
            nxt = (lambda: kn_ref[0:tq, :], "next", j < last)
        else:
            nxt = (lambda: kn_ref[tq:, :], "next", None)
        return [prev, own, nxt, ctx_keys]

    def values_of(i):
        prev = (lambda: vp_ref[...]) if i == 0 else cur_v(i - 1)
        nxt = cur_v(i + 1) if i < n - 1 else (lambda: vn_ref[:, 0:tq])
        return [prev, cur_v(i), nxt, ctx_vals]

    @pl.when(j == 0)
    def _():
        _interleave(scores(lambda: q_ref[:, 0:tq], keys_of(0), buf_a))

    bufs = (buf_a, buf_b)
    for i in range(n):
        if i + 1 < n:
            load_q = functools.partial(lambda i0: q_ref[:, i0 * tq:(i0 + 1) * tq], i + 1)
        else:
            load_q = lambda: qn_ref[...]
        _interleave(scores(load_q, keys_of(i + 1), bufs[(i + 1) % 2]),
                    attend(values_of(i), bufs[i % 2], i * tq))


def _win_attn(qa, ka, va, kac, vac, ga, sink_row, windowed):
    bn, _, t = qa.shape
    ctx_len = kac.shape[1]
    kx_spec = pl.BlockSpec((None, ctx_len, A_KV_WIDTH), lambda b, j: (b, 0, 0))
    vx_spec = pl.BlockSpec((None, A_KV_WIDTH, ctx_len), lambda b, j: (b, 0, 0))
    if windowed:
        tq = Q_BLOCK
        nb = t // tq
        blocks = WIN_BLOCKS if nb % WIN_BLOCKS == 0 else 2
        assert blocks % 2 == 0 and nb % blocks == 0
        steps = nb // blocks
        n_keys = 3 * tq + ctx_len
        before = lambda j: jnp.maximum(blocks * j - 1, 0)
        after = lambda j: jnp.minimum((j + 1) * (blocks // 2), nb // 2 - 1)
        pair_spec = pl.BlockSpec((None, A_WIDTH, blocks * tq), lambda b, j: (b, 0, j))
        in_specs = [pair_spec,
                    pl.BlockSpec((None, A_WIDTH, tq), lambda b, j: (b, 0, jnp.minimum(blocks * (j + 1), nb - 1))),
                    pl.BlockSpec((None, tq, A_KV_WIDTH), lambda b, j: (b, before(j), 0)),
                    pl.BlockSpec((None, blocks * tq, A_KV_WIDTH), lambda b, j: (b, j, 0)),
                    pl.BlockSpec((None, 2 * tq, A_KV_WIDTH), lambda b, j: (b, after(j), 0)),
                    kx_spec,
                    pl.BlockSpec((None, A_KV_WIDTH, tq), lambda b, j: (b, 0, before(j))),
                    pl.BlockSpec((None, A_KV_WIDTH, blocks * tq), lambda b, j: (b, 0, j)),
                    pl.BlockSpec((None, A_KV_WIDTH, 2 * tq), lambda b, j: (b, 0, after(j))),
                    vx_spec, pair_spec]
        args = [qa, qa, ka, ka, ka, kac, va, va, va, vac, ga, sink_row]
        out_spec = pair_spec
    else:
        tq = t
        steps = 1
        blocks = 1
        n_keys = ctx_len
        out_spec = pl.BlockSpec((None, A_WIDTH, tq), lambda b, j: (b, 0, 0))
        in_specs = [out_spec, kx_spec, vx_spec, out_spec]
        args = [qa, kac, vac, ga, sink_row]
    width = A_HEADS * tq
    return pl.pallas_call(
        functools.partial(_win_attn_kernel, tq=tq, windowed=windowed, blocks=blocks),
        grid=(bn, steps),
        in_specs=in_specs + [pl.BlockSpec((1, width), lambda b, j: (0, 0))],
        out_specs=out_spec,
        out_shape=jax.ShapeDtypeStruct((bn, A_WIDTH, t), BF16),
        scratch_shapes=[pltpu.VMEM((A_KV_WIDTH, width), BF16), pltpu.VMEM((A_KV_WIDTH, width), BF16),
                        pltpu.VMEM((n_keys, width), F32), pltpu.VMEM((n_keys, width), F32),
                        pltpu.VMEM((1, width), F32), pltpu.VMEM((1, width), F32)],
        compiler_params=_params("arbitrary", "arbitrary"),
        name="win_attn" if windowed else "ctx_sink_attn",
    )(*args)


def _key_chunks(segments, key_chunk):
    chunks = []
    base = 0
    for n, load_k, load_v in segments:
        for c0 in range(0, n, key_chunk):
            size = min(key_chunk, n - c0)
            chunks.append((functools.partial(load_k, c0, size), functools.partial(load_v, c0, size),
                           base + c0, size))
        base += n
    return chunks


def _diff_scores(q, chunks, buf, tq):
    qpad_ref, s_ref, m_ref = buf
    qpad_ref[...] = jnp.zeros_like(qpad_ref)
    qpad_ref[0:HEAD_DIM, 0:tq] = q[0:HEAD_DIM, :]
    qpad_ref[HEAD_DIM:, tq:] = q[HEAD_DIM:, :]
    qpad = qpad_ref[...]
    m = None
    for load_k, _, row, size in chunks:
        s = _dot(load_k(), qpad)
        s_ref[row:row + size, :] = s
        s_max = jnp.max(s, axis=0, keepdims=True)
        m = s_max if m is None else jnp.maximum(m, s_max)
        yield
    m_ref[...] = m


def _diff_attend(chunks, buf, tq, lam, emit):
    _, s_ref, m_ref = buf
    m = m_ref[...]
    l = jnp.zeros((1, 2 * tq), F32)
    acc = jnp.zeros((B_VDIM, 2 * tq), F32)
    for _, load_v, row, size in chunks:
        p = jnp.exp2(s_ref[row:row + size, :] - m)
        l = l + jnp.sum(p, axis=0, keepdims=True)
        p = p.astype(BF16)
        yield
        acc = acc + _dot(load_v(), p)
    acc = acc * (1.0 / l)
    emit(acc[:, 0:tq] - lam * acc[:, tq:])


def _diff_emit(g_ref, w_ref, o_ref, rows, cols):
    def emit(o):
        r = lax.rsqrt(jnp.mean(o * o, axis=0, keepdims=True) + EPS)
        o = (o * r) * w_ref[...]
        o_ref[rows, cols] = (o * _silu(g_ref[rows, cols].astype(F32))).astype(BF16)
    return emit


def _diff_attn_kernel(q_ref, qn_ref, k_ref, kx_ref, kz_ref, kxz_ref, v_ref, vx_ref,
                      g_ref, w_ref, lam_ref, o_ref, qpad_a, qpad_b, s_a, s_b, m_a, m_b,
                      *, tq, key_chunk, tiles):
    buf_a = (qpad_a, s_a, m_a)
    buf_b = (qpad_b, s_b, m_b)

    def chunks_of(keys, ctx_keys):
        return _key_chunks(
            [(keys.shape[0], lambda c0, n: keys[c0:c0 + n, :], lambda c0, n: v_ref[:, c0:c0 + n]),
             (ctx_keys.shape[0], lambda c0, n: ctx_keys[c0:c0 + n, :], lambda c0, n: vx_ref[:, c0:c0 + n])],
            key_chunk)

    cur = chunks_of(k_ref, kx_ref)
    ahead = chunks_of(kz_ref, kxz_ref)
    lam = lam_ref[...]
    rows = slice(0, B_VDIM)

    @pl.when((pl.program_id(0) == 0) & (pl.program_id(1) == 0) & (pl.program_id(2) == 0))
    def _():
        _interleave(_diff_scores(q_ref[:, 0:tq], cur, buf_a, tq))

    bufs = (buf_a, buf_b)
    for i in range(tiles):
        cols = slice(i * tq, (i + 1) * tq)
        if i + 1 < tiles:
            scores = _diff_scores(q_ref[:, (i + 1) * tq:(i + 2) * tq], cur, bufs[(i + 1) % 2], tq)
        else:
            scores = _diff_scores(qn_ref[...], ahead, bufs[(i + 1) % 2], tq)
        _interleave(scores, _diff_attend(cur, bufs[i % 2], tq, lam,
                                         _diff_emit(g_ref, w_ref, o_ref, rows, cols)))


def _diff_attn(qb, kb, vb, kbc, vbc, gb, w_sub, lam_row, tq):
    bn, _, t = qb.shape
    t_keys = kb.shape[1]
    ctx_len = kbc.shape[1]
    tiles = DIFF_TILES if t % (DIFF_TILES * tq) == 0 else 2
    assert tiles % 2 == 0 and t % (tiles * tq) == 0
    nq = t // (tiles * tq)

    def following(b, h):
        b_next = jnp.minimum(b + (h + 1) // B_HEADS, bn - 1)
        return b_next, jnp.where(b + (h + 1) // B_HEADS > bn - 1, h, (h + 1) % B_HEADS)

    def next_tile(b, h, j):
        b_next, h_next = following(b, h)
        wrap = j == nq - 1
        return jnp.where(wrap, b_next, b), jnp.where(wrap, h_next, h), jnp.where(wrap, 0, tiles * (j + 1))

    def keys_ahead(b, h, j):
        b_next, h_next, _ = next_tile(b, h, j)
        return b_next, 0, h_next

    head_fm = lambda width: pl.BlockSpec((None, B_VDIM, width), lambda b, h, j: (b, h, 0))
    pair_spec = pl.BlockSpec((None, B_VDIM, tiles * tq), lambda b, h, j: (b, h, j))
    in_specs = [
        pair_spec,
        pl.BlockSpec((None, B_VDIM, tq), lambda b, h, j: next_tile(b, h, j)),
        pl.BlockSpec((None, t_keys, B_VDIM), lambda b, h, j: (b, 0, h)),
        pl.BlockSpec((None, ctx_len, B_VDIM), lambda b, h, j: (b, 0, h)),
        pl.BlockSpec((None, t_keys, B_VDIM), lambda b, h, j: keys_ahead(b, h, j)),
        pl.BlockSpec((None, ctx_len, B_VDIM), lambda b, h, j: keys_ahead(b, h, j)),
        head_fm(t_keys), head_fm(ctx_len), pair_spec,
        pl.BlockSpec((B_VDIM, tq), lambda b, h, j: (0, 0)),
        pl.BlockSpec((1, tq), lambda b, h, j: (0, 0)),
    ]
    n_keys = t_keys + ctx_len
    return pl.pallas_call(
        functools.partial(_diff_attn_kernel, tq=tq, key_chunk=512, tiles=tiles),
        grid=(bn, B_HEADS, nq),
        in_specs=in_specs,
        out_specs=pair_spec,
        out_shape=jax.ShapeDtypeStruct((bn, B_WIDTH, t), BF16),
        scratch_shapes=[pltpu.VMEM((B_VDIM, 2 * tq), BF16), pltpu.VMEM((B_VDIM, 2 * tq), BF16),
                        pltpu.VMEM((n_keys, 2 * tq), F32), pltpu.VMEM((n_keys, 2 * tq), F32),
                        pltpu.VMEM((1, 2 * tq), F32), pltpu.VMEM((1, 2 * tq), F32)],
        compiler_params=_params("arbitrary", "arbitrary", "arbitrary"),
        name="diff_attn",
    )(qb, qb, kb, kbc, kb, kbc, vb, vbc, gb, w_sub, lam_row)


def _ctx_diff_attn_kernel(q_ref, kx_ref, vx_ref, g_ref, w_ref, lam_ref, o_ref,
                          qpad_a, qpad_b, s_a, s_b, m_a, m_b, *, tq, key_chunk):
    bufs = [(qpad_a, s_a, m_a), (qpad_b, s_b, m_b)]
    lam = lam_ref[...]
    n = kx_ref.shape[0]

    def head(h):
        rows = slice(B_VDIM * h, B_VDIM * (h + 1))
        chunks = _key_chunks([(n, lambda c0, size: kx_ref[c0:c0 + size, rows],
                               lambda c0, size: vx_ref[rows, c0:c0 + size])], key_chunk)
        return rows, chunks

    rows, chunks = head(0)
    _interleave(_diff_scores(q_ref[rows, :], chunks, bufs[0], tq))
    for h in range(B_HEADS):
        rows, chunks = head(h)
        gens = [_diff_attend(chunks, bufs[h % 2], tq, lam,
                             _diff_emit(g_ref, w_ref, o_ref, rows, slice(0, tq)))]
        if h + 1 < B_HEADS:
            rows_n, chunks_n = head(h + 1)
            gens.insert(0, _diff_scores(q_ref[rows_n, :], chunks_n, bufs[(h + 1) % 2], tq))
        _interleave(*gens)


def _ctx_diff_attn(qbc, kbc, vbc, gbc, w_sub, lam_row):
    bn, _, tq = qbc.shape
    n_keys = kbc.shape[1]
    fm = pl.BlockSpec((None, B_WIDTH, tq), lambda b: (b, 0, 0))
    return pl.pallas_call(
        functools.partial(_ctx_diff_attn_kernel, tq=tq, key_chunk=512),
        grid=(bn,),
        in_specs=[fm, pl.BlockSpec((None, n_keys, B_WIDTH), lambda b: (b, 0, 0)),
                  pl.BlockSpec((None, B_WIDTH, n_keys), lambda b: (b, 0, 0)), fm,
                  pl.BlockSpec((B_VDIM, tq), lambda b: (0, 0)),
                  pl.BlockSpec((1, tq), lambda b: (0, 0))],
        out_specs=fm,
        out_shape=jax.ShapeDtypeStruct((bn, B_WIDTH, tq), BF16),
        scratch_shapes=[pltpu.VMEM((B_VDIM, 2 * tq), BF16), pltpu.VMEM((B_VDIM, 2 * tq), BF16),
                        pltpu.VMEM((n_keys, 2 * tq), F32), pltpu.VMEM((n_keys, 2 * tq), F32),
                        pltpu.VMEM((1, 2 * tq), F32), pltpu.VMEM((1, 2 * tq), F32)],
        compiler_params=_params("arbitrary"),
        name="ctx_diff_attn",
    )(qbc, kbc, vbc, gbc, w_sub, lam_row)


def _out_proj_kernel(*refs, latent):
    (oa_ref, ob_ref, w0_ref, x_ref, gt_ref, g_ref, sc_ref, sh_ref, w_ref, wab_ref, bab_ref) = refs[:11]
    if latent:
        xo_ref, q_ref, k_ref, v_ref, gate_ref, cumf_ref, cumb_ref = refs[11:]
    else:
        k_ref, v_ref, cumf_ref, cumb_ref = refs[11:]
    y = _dot_tn(oa_ref[...], w0_ref[0:A_WIDTH, :]) + _dot_tn(ob_ref[...], w0_ref[A_WIDTH:, :])
    x_new = x_ref[...] + gt_ref[...] * y
    h = _modulate(x_new, g_ref, sc_ref, sh_ref)

    def cols(start, size):
        return _dot(h, w_ref[:, start:start + size])

    low = cols(ODD_MAIN, 128).astype(BF16)
    k_ref[...] = cols(G_QK, G_QK).astype(BF16)
    z = _dot(low, wab_ref[...]) + bab_ref[...]
    la = (jnp.minimum(z, 0.0) - jnp.log(1.0 + jnp.exp(-jnp.abs(z)))) / G_TAU
    for i in range(G_V // 512):
        v_ref[:, 512 * i:512 * (i + 1)] = cols(2 * G_QK + 512 * i, 512).astype(BF16)
    if latent:
        xo_ref[...] = x_new
        q_ref[...] = (cols(0, G_QK) * (G_DK ** -0.5)).astype(BF16)
        for i in range(G_V // 512):
            gate_ref[:, 512 * i:512 * (i + 1)] = _silu(cols(2 * G_QK + G_V + 512 * i, 512)).astype(BF16)
    r = lax.broadcasted_iota(jnp.int32, (G_BLOCK, G_BLOCK), 0)
    c = lax.broadcasted_iota(jnp.int32, (G_BLOCK, G_BLOCK), 1)
    in_chunk = r % G_CHUNK
    tri_f = jnp.where((r - c).astype(jnp.uint32) <= in_chunk.astype(jnp.uint32), 1.0, 0.0)
    tri_b = jnp.where((c - r).astype(jnp.uint32) <= (G_CHUNK - 1 - in_chunk).astype(jnp.uint32),
                      1.0, 0.0)
    for blk in range(la.shape[0] // G_BLOCK):
        rows = slice(blk * G_BLOCK, (blk + 1) * G_BLOCK)
        for tri, lo_col, out_ref in ((tri_f, 0, cumf_ref), (tri_b, G_QK, cumb_ref)):
            hi, lo = _split_bf16(la[rows, lo_col:lo_col + G_QK])
            cum = _dot(tri.astype(BF16), jnp.concatenate([hi, lo], axis=1))
            out_ref[rows, :] = cum[:, :G_QK] + cum[:, G_QK:]


def _out_proj(oa, ob, w0, x, gate, norm_g, sc, sh, w, wab, bab, latent, tm):
    bn, t, _ = x.shape
    fm = pl.BlockSpec((None, A_WIDTH, tm), lambda j, b: (b, 0, j))

    def tk(width, dtype):
        return (pl.BlockSpec((None, tm, width), lambda j, b: (b, j, 0)),
                jax.ShapeDtypeStruct((bn, t, width), dtype))

    const = lambda rows, cols: pl.BlockSpec((rows, cols), lambda j, b: (0, 0))
    x_spec = tk(D_MODEL, F32)
    tail = [tk(G_QK, BF16), tk(G_V, BF16), tk(G_QK, F32), tk(G_QK, F32)]
    if latent:
        outs = [x_spec, tk(G_QK, BF16), tail[0], tail[1], tk(G_V, BF16), tail[2], tail[3]]
    else:
        outs = tail
    return pl.pallas_call(
        functools.partial(_out_proj_kernel, latent=latent),
        grid=(t // tm, bn),
        in_specs=[
            fm, fm, const(D_MODEL, D_MODEL), x_spec[0], _mod_spec(latent),
            const(1, D_MODEL), _mod_spec(latent), _mod_spec(latent),
            const(D_MODEL, ODD_PAD), const(128, 2 * G_QK), const(1, 2 * G_QK),
        ],
        out_specs=[o[0] for o in outs],
        out_shape=[o[1] for o in outs],
        compiler_params=_params("arbitrary", "arbitrary"),
        name="out_proj" if latent else "ctx_out_proj",
    )(oa, ob, w0, x, gate, norm_g, sc, sh, w, wab, bab)


def _gla_block(q, k, v, b, state, reverse, with_out):
    n = G_BLOCK // G_CHUNK
    ends = [(j * G_CHUNK if reverse else (j + 1) * G_CHUNK - 1) for j in range(n)]
    b_last = jnp.concatenate(
        [jnp.broadcast_to(b[e:e + 1, :], (G_CHUNK, G_DK)) for e in ends], axis=0)
    kf = k.astype(F32)
    k_state = (kf * jnp.exp(b_last - b)).astype(BF16)
    if with_out:
        qd = (q.astype(F32) * jnp.exp(b)).astype(BF16)
        kd = (kf * jnp.exp(-b)).astype(BF16)
    yield
    if with_out:
        att = _dot_nt(qd, kd)
        yield
    chunks = [slice(j * G_CHUNK, (j + 1) * G_CHUNK) for j in range(n)]
    d_state = [_dot_tn(k_state[sl], v[sl]) for sl in chunks]
    decay_t = jnp.exp(b_last).T
    yield
    if with_out:
        r = lax.broadcasted_iota(jnp.int32, (G_BLOCK, G_BLOCK), 0)
        c = lax.broadcasted_iota(jnp.int32, (G_BLOCK, G_BLOCK), 1)
        in_chunk = r % G_CHUNK
        if reverse:
            causal = (c - r).astype(jnp.uint32) <= (G_CHUNK - 1 - in_chunk).astype(jnp.uint32)
        else:
            causal = (r - c).astype(jnp.uint32) <= in_chunk.astype(jnp.uint32)
        intra = _dot(jnp.where(causal, att, 0.0).astype(BF16), v)
        yield
    outs = [None] * n
    for j in (reversed(range(n)) if reverse else range(n)):
        if with_out:
            outs[j] = intra[chunks[j]] + _dot(qd[chunks[j]], state.astype(BF16))
        decay = jnp.broadcast_to(decay_t[:, ends[j]:ends[j] + 1], (G_DK, G_DV))
        state = state * decay + d_state[j]
        yield
    return (jnp.concatenate(outs, axis=0) if with_out else None), state


def _gla_kernel(q_ref, k_ref, v_ref, cumf_ref, cumb_ref, g_ref,
                kc_ref, vc_ref, cumfc_ref, cumbc_ref, w_ref, o_ref, of_ref, ob_ref, *, heads):
    n_lat = q_ref.shape[0] // G_BLOCK
    n_ctx = kc_ref.shape[0] // G_BLOCK
    qk = [slice(G_DK * h, G_DK * (h + 1)) for h in range(heads)]
    vv = [slice(G_DV * h, G_DV * (h + 1)) for h in range(heads)]

    def rows(i):
        return pl.ds(pl.multiple_of(i * G_BLOCK, G_BLOCK), G_BLOCK)

    states = [jnp.zeros((G_DK, G_DV), F32)] * (2 * heads)
    for i in range(n_ctx):
        sl_f = slice(i * G_BLOCK, (i + 1) * G_BLOCK)
        sl_b = slice((n_ctx - 1 - i) * G_BLOCK, (n_ctx - i) * G_BLOCK)
        gens = []
        for h in range(heads):
            gens.append(_gla_block(None, kc_ref[sl_f, qk[h]], vc_ref[sl_f, vv[h]],
                                   cumfc_ref[sl_f, qk[h]], states[2 * h], False, False))
            gens.append(_gla_block(None, kc_ref[sl_b, qk[h]], vc_ref[sl_b, vv[h]],
                                   cumbc_ref[sl_b, qk[h]], states[2 * h + 1], True, False))
        states = [st for _, st in _interleave(*gens)]

    def finish(i, h, o, other_ref):
        o = o + other_ref[rows(i), vv[h]]
        rn = lax.rsqrt(jnp.mean(o * o, axis=-1, keepdims=True) + EPS)
        o = (o * rn) * w_ref[...]
        o_ref[rows(i), vv[h]] = (o * g_ref[rows(i), vv[h]].astype(F32)).astype(BF16)

    def body(i, states, second_half):
        i_b = n_lat - 1 - i
        gens = []
        for h in range(heads):
            gens.append(_gla_block(q_ref[rows(i), qk[h]], k_ref[rows(i), qk[h]], v_ref[rows(i), vv[h]],
                                   cumf_ref[rows(i), qk[h]], states[2 * h], False, True))
            gens.append(_gla_block(q_ref[rows(i_b), qk[h]], k_ref[rows(i_b), qk[h]],
                                   v_ref[rows(i_b), vv[h]], cumb_ref[rows(i_b), qk[h]],
                                   states[2 * h + 1], True, True))
        results = _interleave(*gens)
        for h in range(heads):
            (o_f, _), (o_b, _) = results[2 * h], results[2 * h + 1]
            if second_half:
                finish(i, h, o_f, ob_ref)
                finish(i_b, h, o_b, of_ref)
            else:
                of_ref[rows(i), vv[h]] = o_f
                ob_ref[rows(i_b), vv[h]] = o_b
        return tuple(st for _, st in results)

    assert n_lat % 2 == 0
    states = lax.fori_loop(0, n_lat // 2, functools.partial(body, second_half=False),
                           tuple(states), unroll=True)
    lax.fori_loop(n_lat // 2, n_lat, functools.partial(body, second_half=True), states, unroll=True)


def _gla(q, k, v, cumf, cumb, gate, kc, vc, cumfc, cumbc, w_norm):
    bn, t, _ = q.shape
    ctx_len = kc.shape[1]
    heads = GLA_HEADS_PER_STEP
    lat = lambda width: pl.BlockSpec((None, t, heads * width), lambda b, h: (b, 0, h))
    ctx = lambda width: pl.BlockSpec((None, ctx_len, heads * width), lambda b, h: (b, 0, h))
    return pl.pallas_call(
        functools.partial(_gla_kernel, heads=heads),
        grid=(bn, G_HEADS // heads),
        in_specs=[lat(G_DK), lat(G_DK), lat(G_DV), lat(G_DK), lat(G_DK), lat(G_DV),
                  ctx(G_DK), ctx(G_DV), ctx(G_DK), ctx(G_DK),
                  pl.BlockSpec((1, G_DV), lambda b, h: (0, 0))],
        out_specs=lat(G_DV),
        out_shape=jax.ShapeDtypeStruct((bn, t, G_V), BF16),
        scratch_shapes=[pltpu.VMEM((t, heads * G_DV), F32), pltpu.VMEM((t, heads * G_DV), F32)],
        compiler_params=_params("arbitrary", "arbitrary"),
        name="gla",
    )(q, k, v, cumf, cumb, gate, kc, vc, cumfc, cumbc, w_norm)


def _odd_out_kernel(o_ref, w_ref, x_ref, gt_ref, out_ref):
    out_ref[...] = x_ref[...] + gt_ref[...] * _dot(o_ref[...], w_ref[...])


def _odd_out(o, w, x, gate, tm):
    bn, t, _ = x.shape
    xs = pl.BlockSpec((None, tm, D_MODEL), lambda j, b: (b, j, 0))
    return pl.pallas_call(
        _odd_out_kernel,
        grid=(t // tm, bn),
        in_specs=[xs, pl.BlockSpec((D_MODEL, D_MODEL), lambda j, b: (0, 0)), xs, _mod_spec(True)],
        out_specs=xs,
        out_shape=jax.ShapeDtypeStruct(x.shape, F32),
        compiler_params=_params("arbitrary", "arbitrary"),
        name="odd_out",
    )(o, w, x, gate)


TOKEN_TILE = 1024
OUT_PROJ_TILE = 512
FINAL_TILE = 2048


def _token_tile(t):
    return TOKEN_TILE if t % TOKEN_TILE == 0 else t


def kernel(x, c, ctx, c_ctx, adaln_w, adaln_b, norm_g, w_out, ab_w_in, a_q_norm, a_k_norm, a_sink,
           b_q_norm, b_k_norm, b_lambda_q1, b_lambda_k1, b_lambda_q2, b_lambda_k2, b_subln,
           gla_w_in, gla_wa_f, gla_ba_f, gla_wa_b, gla_ba_b, gla_out_norm):
    bn, t, _ = x.shape
    ctx_len = ctx.shape[1]
    depth = adaln_w.shape[0]
    assert depth == 2

    pad_rows = (-(bn + 1)) % 8
    cond = jnp.concatenate([c, c_ctx[None, :], jnp.zeros((pad_rows, D_MODEL), F32)], axis=0)
    mod = _adaln(cond, adaln_w, adaln_b[:, None, :])

    def mods(layer):
        m = mod[layer]
        shift, scale, gate = (m[:, i * D_MODEL:(i + 1) * D_MODEL] for i in range(3))
        per_x = tuple(v[:bn, None, :] for v in (shift, scale, gate))
        per_c = tuple(v[bn:bn + 1, None, :] for v in (shift, scale, gate))
        return per_x, per_c

    (shx, scx, gtx), (shc, scc, gtc) = mods(0)
    lambda_init = 0.8 - 0.6 * math.exp(-0.3 * 0)
    wt = ab_w_in[0].T.astype(BF16)
    gains = [a_q_norm[0], a_k_norm[0], b_q_norm[0], b_k_norm[0]]
    q_scale = [ATTN_SCALE * LOG2E, 1.0, ATTN_SCALE * LOG2E, 1.0]
    tab_x = _rope_tables(gains, t, True, q_scale)
    tab_c = _rope_tables(gains, ctx_len, False, q_scale)
    g0 = norm_g[0][None, :]
    qa, ka, va, ga, qb, kb, vb, gb = _even_proj(x, g0, scx, shx, wt, tab_x, True, _token_tile(t))
    qac, kac, vac, gac, qbc, kbc, vbc, gbc = _even_proj(ctx, g0, scc, shc, wt, tab_c, False,
                                                        _token_tile(ctx_len))

    sink = a_sink[0].astype(F32) * LOG2E
    oa = _win_attn(qa, ka, va, kac, vac, ga, jnp.repeat(sink, Q_BLOCK)[None, :], True)
    oac = _win_attn(qac, None, None, kac, vac, gac, jnp.repeat(sink, ctx_len)[None, :], False)

    lam = (jnp.exp(jnp.sum(b_lambda_q1[0].astype(F32) * b_lambda_k1[0].astype(F32)))
           - jnp.exp(jnp.sum(b_lambda_q2[0].astype(F32) * b_lambda_k2[0].astype(F32))) + lambda_init)
    tq = 256
    w_sub = jnp.broadcast_to((b_subln[0].astype(F32) * (1.0 - lambda_init))[:, None], (B_VDIM, tq))
    lam_row = jnp.broadcast_to(lam, (1, tq)).astype(F32)
    ob = _diff_attn(qb, kb, vb, kbc, vbc, gb, w_sub, lam_row, tq)
    obc = _ctx_diff_attn(qbc, kbc, vbc, gbc, w_sub, lam_row)

    w0 = w_out[0].astype(BF16)
    (shx1, scx1, gtx1), (shc1, scc1, _) = mods(1)
    w1 = jnp.pad(gla_w_in[0], ((0, 0), (0, ODD_PAD - gla_w_in.shape[2]))).astype(BF16)
    wab = jnp.zeros((128, 2 * G_QK), F32)
    wab = wab.at[0:G_RANK, 0:G_QK].set(gla_wa_f[0])
    wab = wab.at[G_RANK:2 * G_RANK, G_QK:].set(gla_wa_b[0]).astype(BF16)
    bab = jnp.concatenate([gla_ba_f[0], gla_ba_b[0]])[None, :].astype(F32)
    g1 = norm_g[1][None, :]
    x, q, k, v, gate, cumf, cumb = _out_proj(oa, ob, w0, x, gtx, g1, scx1, shx1, w1, wab, bab,
                                             True, OUT_PROJ_TILE if t % OUT_PROJ_TILE == 0 else t)
    kc, vc, cumfc, cumbc = _out_proj(oac, obc, w0, ctx, gtc, g1, scc1, shc1, w1, wab, bab,
                                     False, _token_tile(ctx_len))
    o = _gla(q, k, v, cumf, cumb, gate, kc, vc, cumfc, cumbc, gla_out_norm[0][None, :].astype(F32))
    return _odd_out(o, w_out[1].astype(BF16), x, gtx1, FINAL_TILE if t % FINAL_TILE == 0 else t)
```

```python
import functools
import math

import jax
import jax.numpy as jnp
from jax import lax
from jax.experimental import pallas as pl
from jax.experimental.pallas import tpu as pltpu

D_MODEL = 1024
GRID_W = 64
HEAD_DIM = 64
ROPE_BASE = 10000.0
EPS = 1e-6
NEG_INF = -1e30
ATTN_SCALE = HEAD_DIM ** -0.5
LOG2E = math.log2(math.e)
WINDOW = 128
Q_BLOCK = 128

A_HEADS = 8
A_KV_HEADS = 2
A_GROUP = A_HEADS // A_KV_HEADS
A_WIDTH = A_HEADS * HEAD_DIM
A_KV_WIDTH = A_KV_HEADS * HEAD_DIM
B_HEADS = 4
B_VDIM = 2 * HEAD_DIM
B_WIDTH = B_HEADS * B_VDIM
EVEN_IN = 2 * A_WIDTH + 2 * A_KV_WIDTH + 4 * B_WIDTH

G_HEADS = 4
G_DK = 128
G_DV = 256
G_RANK = 16
G_TAU = 16.0
G_CHUNK = 64
G_BLOCK = 256
GLA_HEADS_PER_STEP = 2
WIN_BLOCKS = 8
DIFF_TILES = 4
G_QK = G_HEADS * G_DK
G_V = G_HEADS * G_DV
ODD_MAIN = 2 * G_QK + 2 * G_V
ODD_PAD = ODD_MAIN + 128

VMEM_LIMIT_BYTES = 56 * 1024 * 1024

BF16 = jnp.bfloat16
F32 = jnp.float32


def _params(*semantics):
    return pltpu.CompilerParams(dimension_semantics=semantics,
                                vmem_limit_bytes=VMEM_LIMIT_BYTES)


def _silu(v):
    return v * (1.0 / (1.0 + jnp.exp(-v)))


def _dot(a, b):
    return jnp.dot(a, b, preferred_element_type=F32)


def _dot_nt(a, b):
    return lax.dot_general(a, b, (((1,), (1,)), ((), ())), preferred_element_type=F32)


def _dot_tn(a, b):
    return lax.dot_general(a, b, (((0,), (0,)), ((), ())), preferred_element_type=F32)


def _split_bf16(v):
    hi = v.astype(BF16)
    lo = (v - hi.astype(F32)).astype(BF16)
    return hi, lo


def _adaln_kernel(cond_ref, w_ref, b_ref, o_ref):
    a = _silu(cond_ref[...])
    a_hi, a_lo = _split_bf16(a)
    w_hi, w_lo = _split_bf16(w_ref[...])
    o_ref[...] = (_dot(a_hi, w_hi) + _dot(a_hi, w_lo) + _dot(a_lo, w_hi)) + b_ref[...]


def _adaln(cond, w, b):
    n_layers = w.shape[0]
    rows = cond.shape[0]
    tn = 512
    return pl.pallas_call(
        _adaln_kernel,
        grid=(n_layers, 3 * D_MODEL // tn),
        in_specs=[
            pl.BlockSpec((rows, D_MODEL), lambda l, n: (0, 0)),
            pl.BlockSpec((None, D_MODEL, tn), lambda l, n: (l, 0, n)),
            pl.BlockSpec((None, 1, tn), lambda l, n: (l, 0, n)),
        ],
        out_specs=pl.BlockSpec((None, rows, tn), lambda l, n: (l, 0, n)),
        out_shape=jax.ShapeDtypeStruct((n_layers, rows, 3 * D_MODEL), F32),
        compiler_params=_params("arbitrary", "arbitrary"),
        name="adaln",
    )(cond, w, b)


def _modulate(xf, g_ref, sc_ref, sh_ref):
    r = lax.rsqrt(jnp.mean(xf * xf, axis=-1, keepdims=True) + EPS)
    return ((xf * r) * g_ref[...] * (1.0 + sc_ref[...]) + sh_ref[...]).astype(BF16)


def _modulated(x_ref, g_ref, sc_ref, sh_ref):
    return _modulate(x_ref[...], g_ref, sc_ref, sh_ref)


def _interleave(*gens):
    results = [None] * len(gens)
    live = list(range(len(gens)))
    while live:
        for idx in list(live):
            try:
                next(gens[idx])
            except StopIteration as stop:
                results[idx] = stop.value
                live.remove(idx)
    return results


def _mod_spec(per_batch):
    if per_batch:
        return pl.BlockSpec((None, 1, D_MODEL), lambda j, b: (b, 0, 0))
    return pl.BlockSpec((None, 1, D_MODEL), lambda j, b: (0, 0, 0))


def _rope_head(blk, c, s):
    r = lax.rsqrt(jnp.mean(blk * blk, axis=0, keepdims=True) + EPS)
    partner = jnp.concatenate([blk[16:32], blk[0:16], blk[48:64], blk[32:48]], axis=0)
    return (blk * c + partner * s) * r


def _even_proj_kernel(x_ref, g_ref, sc_ref, sh_ref, wt_ref, tab_ref,
                      qa_ref, ka_ref, va_ref, ga_ref, qb_ref, kb_ref, vb_ref, gb_ref):
    h = _modulated(x_ref, g_ref, sc_ref, sh_ref)

    def rows(start, size):
        return _dot_nt(wt_ref[start:start + size, :], h)

    def normed(acc, table):
        c = tab_ref[2 * table]
        s = tab_ref[2 * table + 1]
        n = acc.shape[0] // HEAD_DIM
        return jnp.concatenate(
            [_rope_head(acc[HEAD_DIM * i:HEAD_DIM * (i + 1)], c, s) for i in range(n)], axis=0)

    off = 0
    qa_ref[...] = normed(rows(off, A_WIDTH), 0).astype(BF16)
    off += A_WIDTH
    kv = rows(off, 2 * A_KV_WIDTH)
    ka_ref[...] = normed(kv[0:A_KV_WIDTH], 1).T.astype(BF16)
    va_ref[...] = kv[A_KV_WIDTH:].astype(BF16)
    off += 2 * A_KV_WIDTH
    ga_ref[...] = rows(off, A_WIDTH).astype(BF16)
    off += A_WIDTH
    qb_ref[...] = normed(rows(off, B_WIDTH), 2).astype(BF16)
    off += B_WIDTH
    kb_ref[...] = normed(rows(off, B_WIDTH), 3).T.astype(BF16)
    off += B_WIDTH
    vb_ref[...] = rows(off, B_WIDTH).astype(BF16)
    off += B_WIDTH
    gb_ref[...] = rows(off, B_WIDTH).astype(BF16)


def _even_proj(x, norm_g, sc, sh, wt, tables, per_batch, tm):
    bn, t, _ = x.shape
    nt = t // tm

    def fm(width):
        return (pl.BlockSpec((None, width, tm), lambda j, b: (b, 0, j)),
                jax.ShapeDtypeStruct((bn, width, t), BF16))

    def tk(width):
        return (pl.BlockSpec((None, tm, width), lambda j, b: (b, j, 0)),
                jax.ShapeDtypeStruct((bn, t, width), BF16))

    outs = [fm(A_WIDTH), tk(A_KV_WIDTH), fm(A_KV_WIDTH), fm(A_WIDTH),
            fm(B_WIDTH), tk(B_WIDTH), fm(B_WIDTH), fm(B_WIDTH)]
    return pl.pallas_call(
        _even_proj_kernel,
        grid=(nt, bn),
        in_specs=[
            pl.BlockSpec((None, tm, D_MODEL), lambda j, b: (b, j, 0)),
            pl.BlockSpec((1, D_MODEL), lambda j, b: (0, 0)),
            _mod_spec(per_batch), _mod_spec(per_batch),
            pl.BlockSpec((EVEN_IN, D_MODEL), lambda j, b: (0, 0)),
            pl.BlockSpec((8, HEAD_DIM, tm), lambda j, b: (0, 0, j)),
        ],
        out_specs=[o[0] for o in outs],
        out_shape=[o[1] for o in outs],
        compiler_params=_params("arbitrary", "arbitrary"),
        name="even_proj",
    )(x, norm_g, sc, sh, wt, tables)


def _rope_tables(gains, t, rotary, q_scale):
    m = HEAD_DIM // 4
    inv = ROPE_BASE ** (-jnp.arange(m, dtype=F32) / m)
    pos = jnp.arange(t, dtype=jnp.int32)
    rows = (pos // GRID_W).astype(F32)
    cols = (pos % GRID_W).astype(F32)
    if rotary:
        ang_r = inv[:, None] * rows[None, :]
        ang_c = inv[:, None] * cols[None, :]
        cos = jnp.concatenate([jnp.cos(ang_r), jnp.cos(ang_r), jnp.cos(ang_c), jnp.cos(ang_c)], axis=0)
        sin = jnp.concatenate([-jnp.sin(ang_r), jnp.sin(ang_r), -jnp.sin(ang_c), jnp.sin(ang_c)], axis=0)
    else:
        cos = jnp.ones((HEAD_DIM, t), F32)
        sin = jnp.zeros((HEAD_DIM, t), F32)
    out = []
    for g, scale in zip(gains, q_scale):
        g = g.astype(F32)
        gp = jnp.concatenate([g[16:32], g[0:16], g[48:64], g[32:48]])
        out.append(g[:, None] * cos * scale)
        out.append(gp[:, None] * sin * scale)
    return jnp.stack(out)


def _win_attn_kernel(*refs, tq, windowed, blocks):
    width = A_HEADS * tq
    if windowed:
        (q_ref, qn_ref, kp_ref, kc_ref, kn_ref, kx_ref, vp_ref, vc_ref, vn_ref, vx_ref,
         g_ref, sink_ref, o_ref, qpad_a, qpad_b, s_a, s_b, m_a, m_b) = refs
    else:
        (q_ref, kx_ref, vx_ref, g_ref, sink_ref, o_ref, qpad_a, qpad_b, s_a, s_b, m_a, m_b) = refs
    buf_a = (qpad_a, s_a, m_a)
    buf_b = (qpad_b, s_b, m_b)

    def scores(load_q, key_segs, buf):
        qpad_ref, s_ref, m_ref = buf
        q = load_q()
        qpad_ref[...] = jnp.zeros_like(qpad_ref)
        for hd in range(A_HEADS):
            grp = hd // A_GROUP
            qpad_ref[HEAD_DIM * grp:HEAD_DIM * (grp + 1), tq * hd:tq * (hd + 1)] = (
                q[HEAD_DIM * hd:HEAD_DIM * (hd + 1), :])
        qpad = qpad_ref[...]
        m = sink_ref[...]
        row = 0
        for load_key, kind, valid in key_segs:
            s = _dot(load_key(), qpad)
            n = s.shape[0]
            if kind is not None:
                r = lax.broadcasted_iota(jnp.int32, s.shape, 0)
                cq = lax.broadcasted_iota(jnp.int32, s.shape, 1) % tq
                off = 0 if valid is None else jnp.where(valid, 0, tq)
                ok = (r >= cq + off) if kind == "prev" else (r <= cq - off)
                s = jnp.where(ok, s, NEG_INF)
            s_ref[row:row + n, :] = s
            m = jnp.maximum(m, jnp.max(s, axis=0, keepdims=True))
            row += n
            yield
        m_ref[...] = m

    def attend(value_segs, buf, col0):
        _, s_ref, m_ref = buf
        m = m_ref[...]
        l = jnp.exp2(sink_ref[...] - m)
        acc = jnp.zeros((A_KV_WIDTH, width), F32)
        row = 0
        for load_value in value_segs:
            v = load_value()
            n = v.shape[1]
            p = jnp.exp2(s_ref[row:row + n, :] - m)
            l = l + jnp.sum(p, axis=0, keepdims=True)
            p = p.astype(BF16)
            yield
            acc = acc + _dot(v, p)
            row += n
        inv = 1.0 / l
        for hd in range(A_HEADS):
            grp = hd // A_GROUP
            o = acc[HEAD_DIM * grp:HEAD_DIM * (grp + 1), tq * hd:tq * (hd + 1)]
            o = o * inv[:, tq * hd:tq * (hd + 1)]
            gate = g_ref[HEAD_DIM * hd:HEAD_DIM * (hd + 1), col0:col0 + tq].astype(F32)
            o_ref[HEAD_DIM * hd:HEAD_DIM * (hd + 1), col0:col0 + tq] = (o * _silu(gate)).astype(BF16)

    ctx_keys = (lambda: kx_ref[...], None, None)
    ctx_vals = lambda: vx_ref[...]
    if not windowed:
        _interleave(scores(lambda: q_ref[...], [ctx_keys], buf_a))
        _interleave(attend([ctx_vals], buf_a, 0))
        return

    j = pl.program_id(1)
    last = pl.num_programs(1) - 1
    n = blocks

    def cur(i):
        return lambda: kc_ref[i * tq:(i + 1) * tq, :]

    def cur_v(i):
        return lambda: vc_ref[:, i * tq:(i + 1) * tq]

    def keys_of(i):
        if i == 0:
            prev = (lambda: kp_ref[...], "prev", j > 0)
        else:
            prev = (cur(i - 1), "prev", None)
        if i < n:
            own = (cur(i), None, None)
        else:
            own = (lambda: kn_ref[0:tq, :], None, None)
        if i < n - 1:
            nxt = (cur(i + 1), "next", None)
        elif i == n - 1:
            nxt = (lambda: kn_ref[0:tq, :], "next", j < last)
        else:
            nxt = (lambda: kn_ref[tq:, :], "next", None)
        return [prev, own, nxt, ctx_keys]

    def values_of(i):
        prev = (lambda: vp_ref[...]) if i == 0 else cur_v(i - 1)
        nxt = cur_v(i + 1) if i < n - 1 else (lambda: vn_ref[:, 0:tq])
        return [prev, cur_v(i), nxt, ctx_vals]

    @pl.when(j == 0)
    def _():
        _interleave(scores(lambda: q_ref[:, 0:tq], keys_of(0), buf_a))

    bufs = (buf_a, buf_b)
    for i in range(n):
        if i + 1 < n:
            load_q = functools.partial(lambda i0: q_ref[:, i0 * tq:(i0 + 1) * tq], i + 1)
        else:
            load_q = lambda: qn_ref[...]
        _interleave(scores(load_q, keys_of(i + 1), bufs[(i + 1) % 2]),
                    attend(values_of(i), bufs[i % 2], i * tq))


def _win_attn(qa, ka, va, kac, vac, ga, sink_row, windowed):
    bn, _, t = qa.shape
    ctx_len = kac.shape[1]
    kx_spec = pl.BlockSpec((None, ctx_len, A_KV_WIDTH), lambda b, j: (b, 0, 0))
    vx_spec = pl.BlockSpec((None, A_KV_WIDTH, ctx_len), lambda b, j: (b, 0, 0))
    if windowed:
        tq = Q_BLOCK
        nb = t // tq
        blocks = WIN_BLOCKS if nb % WIN_BLOCKS == 0 else 2
        assert blocks % 2 == 0 and nb % blocks == 0
        steps = nb // blocks
        n_keys = 3 * tq + ctx_len
        before = lambda j: jnp.maximum(blocks * j - 1, 0)
        after = lambda j: jnp.minimum((j + 1) * (blocks // 2), nb // 2 - 1)
        pair_spec = pl.BlockSpec((None, A_WIDTH, blocks * tq), lambda b, j: (b, 0, j))
        in_specs = [pair_spec,
                    pl.BlockSpec((None, A_WIDTH, tq), lambda b, j: (b, 0, jnp.minimum(blocks * (j + 1), nb - 1))),
                    pl.BlockSpec((None, tq, A_KV_WIDTH), lambda b, j: (b, before(j), 0)),
                    pl.BlockSpec((None, blocks * tq, A_KV_WIDTH), lambda b, j: (b, j, 0)),
                    pl.BlockSpec((None, 2 * tq, A_KV_WIDTH), lambda b, j: (b, after(j), 0)),
                    kx_spec,
                    pl.BlockSpec((None, A_KV_WIDTH, tq), lambda b, j: (b, 0, before(j))),
                    pl.BlockSpec((None, A_KV_WIDTH, blocks * tq), lambda b, j: (b, 0, j)),
                    pl.BlockSpec((None, A_KV_WIDTH, 2 * tq), lambda b, j: (b, 0, after(j))),
                    vx_spec, pair_spec]
        args = [qa, qa, ka, ka, ka, kac, va, va, va, vac, ga, sink_row]
        out_spec = pair_spec
    else:
        tq = t
        steps = 1
        blocks = 1
        n_keys = ctx_len
        out_spec = pl.BlockSpec((None, A_WIDTH, tq), lambda b, j: (b, 0, 0))
        in_specs = [out_spec, kx_spec, vx_spec, out_spec]
        args = [qa, kac, vac, ga, sink_row]
    width = A_HEADS * tq
    return pl.pallas_call(
        functools.partial(_win_attn_kernel, tq=tq, windowed=windowed, blocks=blocks),
        grid=(bn, steps),
        in_specs=in_specs + [pl.BlockSpec((1, width), lambda b, j: (0, 0))],
        out_specs=out_spec,
        out_shape=jax.ShapeDtypeStruct((bn, A_WIDTH, t), BF16),
        scratch_shapes=[pltpu.VMEM((A_KV_WIDTH, width), BF16), pltpu.VMEM((A_KV_WIDTH, width), BF16),
                        pltpu.VMEM((n_keys, width), F32), pltpu.VMEM((n_keys, width), F32),
                        pltpu.VMEM((1, width), F32), pltpu.VMEM((1, width), F32)],
        compiler_params=_params("arbitrary", "arbitrary"),
        name="win_attn" if windowed else "ctx_sink_attn",
    )(*args)


def _key_chunks(segments, key_chunk):
    chunks = []
    base = 0
    for n, load_k, load_v in segments:
        for c0 in range(0, n, key_chunk):
            size = min(key_chunk, n - c0)
            chunks.append((functools.partial(load_k, c0, size), functools.partial(load_v, c0, size),
                           base + c0, size))
        base += n
    return chunks


def _diff_scores(q, chunks, buf, tq):
    qpad_ref, s_ref, m_ref = buf
    qpad_ref[...] = jnp.zeros_like(qpad_ref)
    qpad_ref[0:HEAD_DIM, 0:tq] = q[0:HEAD_DIM, :]
    qpad_ref[HEAD_DIM:, tq:] = q[HEAD_DIM:, :]
    qpad = qpad_ref[...]
    m = None
    for load_k, _, row, size in chunks:
        s = _dot(load_k(), qpad)
        s_ref[row:row + size, :] = s
        s_max = jnp.max(s, axis=0, keepdims=True)
        m = s_max if m is None else jnp.maximum(m, s_max)
        yield
    m_ref[...] = m


def _diff_attend(chunks, buf, tq, lam, emit):
    _, s_ref, m_ref = buf
    m = m_ref[...]
    l = jnp.zeros((1, 2 * tq), F32)
    acc = jnp.zeros((B_VDIM, 2 * tq), F32)
    for _, load_v, row, size in chunks:
        p = jnp.exp2(s_ref[row:row + size, :] - m)
        l = l + jnp.sum(p, axis=0, keepdims=True)
        p = p.astype(BF16)
        yield
        acc = acc + _dot(load_v(), p)
    acc = acc * (1.0 / l)
    emit(acc[:, 0:tq] - lam * acc[:, tq:])


def _diff_emit(g_ref, w_ref, o_ref, rows, cols):
    def emit(o):
        r = lax.rsqrt(jnp.mean(o * o, axis=0, keepdims=True) + EPS)
        o = (o * r) * w_ref[...]
        o_ref[rows, cols] = (o * _silu(g_ref[rows, cols].astype(F32))).astype(BF16)
    return emit


def _diff_attn_kernel(q_ref, qn_ref, k_ref, kx_ref, kz_ref, kxz_ref, v_ref, vx_ref,
                      g_ref, w_ref, lam_ref, o_ref, qpad_a, qpad_b, s_a, s_b, m_a, m_b,
                      *, tq, key_chunk, tiles):
    buf_a = (qpad_a, s_a, m_a)
    buf_b = (qpad_b, s_b, m_b)

    def chunks_of(keys, ctx_keys):
        return _key_chunks(
            [(keys.shape[0], lambda c0, n: keys[c0:c0 + n, :], lambda c0, n: v_ref[:, c0:c0 + n]),
             (ctx_keys.shape[0], lambda c0, n: ctx_keys[c0:c0 + n, :], lambda c0, n: vx_ref[:, c0:c0 + n])],
            key_chunk)

    cur = chunks_of(k_ref, kx_ref)
    ahead = chunks_of(kz_ref, kxz_ref)
    lam = lam_ref[...]
    rows = slice(0, B_VDIM)

    @pl.when((pl.program_id(0) == 0) & (pl.program_id(1) == 0) & (pl.program_id(2) == 0))
    def _():
        _interleave(_diff_scores(q_ref[:, 0:tq], cur, buf_a, tq))

    bufs = (buf_a, buf_b)
    for i in range(tiles):
        cols = slice(i * tq, (i + 1) * tq)
        if i + 1 < tiles:
            scores = _diff_scores(q_ref[:, (i + 1) * tq:(i + 2) * tq], cur, bufs[(i + 1) % 2], tq)
        else:
            scores = _diff_scores(qn_ref[...], ahead, bufs[(i + 1) % 2], tq)
        _interleave(scores, _diff_attend(cur, bufs[i % 2], tq, lam,
                                         _diff_emit(g_ref, w_ref, o_ref, rows, cols)))


def _diff_attn(qb, kb, vb, kbc, vbc, gb, w_sub, lam_row, tq):
    bn, _, t = qb.shape
    t_keys = kb.shape[1]
    ctx_len = kbc.shape[1]
    tiles = DIFF_TILES if t % (DIFF_TILES * tq) == 0 else 2
    assert tiles % 2 == 0 and t % (tiles * tq) == 0
    nq = t // (tiles * tq)

    def following(b, h):
        b_next = jnp.minimum(b + (h + 1) // B_HEADS, bn - 1)
        return b_next, jnp.where(b + (h + 1) // B_HEADS > bn - 1, h, (h + 1) % B_HEADS)

    def next_tile(b, h, j):
        b_next, h_next = following(b, h)
        wrap = j == nq - 1
        return jnp.where(wrap, b_next, b), jnp.where(wrap, h_next, h), jnp.where(wrap, 0, tiles * (j + 1))

    def keys_ahead(b, h, j):
        b_next, h_next, _ = next_tile(b, h, j)
        return b_next, 0, h_next

    head_fm = lambda width: pl.BlockSpec((None, B_VDIM, width), lambda b, h, j: (b, h, 0))
    pair_spec = pl.BlockSpec((None, B_VDIM, tiles * tq), lambda b, h, j: (b, h, j))
    in_specs = [
        pair_spec,
        pl.BlockSpec((None, B_VDIM, tq), lambda b, h, j: next_tile(b, h, j)),
        pl.BlockSpec((None, t_keys, B_VDIM), lambda b, h, j: (b, 0, h)),
        pl.BlockSpec((None, ctx_len, B_VDIM), lambda b, h, j: (b, 0, h)),
        pl.BlockSpec((None, t_keys, B_VDIM), lambda b, h, j: keys_ahead(b, h, j)),
        pl.BlockSpec((None, ctx_len, B_VDIM), lambda b, h, j: keys_ahead(b, h, j)),
        head_fm(t_keys), head_fm(ctx_len), pair_spec,
        pl.BlockSpec((B_VDIM, tq), lambda b, h, j: (0, 0)),
        pl.BlockSpec((1, tq), lambda b, h, j: (0, 0)),
    ]
    n_keys = t_keys + ctx_len
    return pl.pallas_call(
        functools.partial(_diff_attn_kernel, tq=tq, key_chunk=512, tiles=tiles),
        grid=(bn, B_HEADS, nq),
        in_specs=in_specs,
        out_specs=pair_spec,
        out_shape=jax.ShapeDtypeStruct((bn, B_WIDTH, t), BF16),
        scratch_shapes=[pltpu.VMEM((B_VDIM, 2 * tq), BF16), pltpu.VMEM((B_VDIM, 2 * tq), BF16),
                        pltpu.VMEM((n_keys, 2 * tq), F32), pltpu.VMEM((n_keys, 2 * tq), F32),
                        pltpu.VMEM((1, 2 * tq), F32), pltpu.VMEM((1, 2 * tq), F32)],
        compiler_params=_params("arbitrary", "arbitrary", "arbitrary"),
        name="diff_attn",
    )(qb, qb, kb, kbc, kb, kbc, vb, vbc, gb, w_sub, lam_row)


def _ctx_diff_attn_kernel(q_ref, kx_ref, vx_ref, g_ref, w_ref, lam_ref, o_ref,
                          qpad_a, qpad_b, s_a, s_b, m_a, m_b, *, tq, key_chunk):
    bufs = [(qpad_a, s_a, m_a), (qpad_b, s_b, m_b)]
    lam = lam_ref[...]
    n = kx_ref.shape[0]

    def head(h):
        rows = slice(B_VDIM * h, B_VDIM * (h + 1))
        chunks = _key_chunks([(n, lambda c0, size: kx_ref[c0:c0 + size, rows],
                               lambda c0, size: vx_ref[rows, c0:c0 + size])], key_chunk)
        return rows, chunks

    rows, chunks = head(0)
    _interleave(_diff_scores(q_ref[rows, :], chunks, bufs[0], tq))
    for h in range(B_HEADS):
        rows, chunks = head(h)
        gens = [_diff_attend(chunks, bufs[h % 2], tq, lam,
                             _diff_emit(g_ref, w_ref, o_ref, rows, slice(0, tq)))]
        if h + 1 < B_HEADS:
            rows_n, chunks_n = head(h + 1)
            gens.insert(0, _diff_scores(q_ref[rows_n, :], chunks_n, bufs[(h + 1) % 2], tq))
        _interleave(*gens)


def _ctx_diff_attn(qbc, kbc, vbc, gbc, w_sub, lam_row):
    bn, _, tq = qbc.shape
    n_keys = kbc.shape[1]
    fm = pl.BlockSpec((None, B_WIDTH, tq), lambda b: (b, 0, 0))
    return pl.pallas_call(
        functools.partial(_ctx_diff_attn_kernel, tq=tq, key_chunk=512),
        grid=(bn,),
        in_specs=[fm, pl.BlockSpec((None, n_keys, B_WIDTH), lambda b: (b, 0, 0)),
                  pl.BlockSpec((None, B_WIDTH, n_keys), lambda b: (b, 0, 0)), fm,
                  pl.BlockSpec((B_VDIM, tq), lambda b: (0, 0)),
                  pl.BlockSpec((1, tq), lambda b: (0, 0))],
        out_specs=fm,
        out_shape=jax.ShapeDtypeStruct((bn, B_WIDTH, tq), BF16),
        scratch_shapes=[pltpu.VMEM((B_VDIM, 2 * tq), BF16), pltpu.VMEM((B_VDIM, 2 * tq), BF16),
                        pltpu.VMEM((n_keys, 2 * tq), F32), pltpu.VMEM((n_keys, 2 * tq), F32),
                        pltpu.VMEM((1, 2 * tq), F32), pltpu.VMEM((1, 2 * tq), F32)],
        compiler_params=_params("arbitrary"),
        name="ctx_diff_attn",
    )(qbc, kbc, vbc, gbc, w_sub, lam_row)


def _out_proj_chain(rows, refs, latent):
    (oa_ref, ob_ref, w0_ref, x_ref, gt_ref, g_ref, sc_ref, sh_ref, w_ref, wab_ref, bab_ref) = refs[:11]
    if latent:
        xo_ref, q_ref, k_ref, v_ref, gate_ref, cumf_ref, cumb_ref = refs[11:]
    else:
        k_ref, v_ref, cumf_ref, cumb_ref = refs[11:]
    y = _dot_tn(oa_ref[:, rows], w0_ref[0:A_WIDTH, :]) + _dot_tn(ob_ref[:, rows], w0_ref[A_WIDTH:, :])
    yield
    x_new = x_ref[rows, :] + gt_ref[...] * y
    h = _modulate(x_new, g_ref, sc_ref, sh_ref)

    def cols(start, size):
        return _dot(h, w_ref[:, start:start + size])

    low = cols(ODD_MAIN, 128).astype(BF16)
    yield
    k_ref[rows, :] = cols(G_QK, G_QK).astype(BF16)
    yield
    z = _dot(low, wab_ref[...]) + bab_ref[...]
    la = (jnp.minimum(z, 0.0) - jnp.log(1.0 + jnp.exp(-jnp.abs(z)))) / G_TAU
    yield
    for i in range(G_V // 512):
        v_ref[rows, 512 * i:512 * (i + 1)] = cols(2 * G_QK + 512 * i, 512).astype(BF16)
        yield
    if latent:
        xo_ref[rows, :] = x_new
        q_ref[rows, :] = (cols(0, G_QK) * (G_DK ** -0.5)).astype(BF16)
        yield
        for i in range(G_V // 512):
            gate_ref[rows, 512 * i:512 * (i + 1)] = _silu(cols(2 * G_QK + G_V + 512 * i, 512)).astype(BF16)
            yield
    r = lax.broadcasted_iota(jnp.int32, (G_BLOCK, G_BLOCK), 0)
    c = lax.broadcasted_iota(jnp.int32, (G_BLOCK, G_BLOCK), 1)
    in_chunk = r % G_CHUNK
    tri_f = jnp.where((r - c).astype(jnp.uint32) <= in_chunk.astype(jnp.uint32), 1.0, 0.0)
    tri_b = jnp.where((c - r).astype(jnp.uint32) <= (G_CHUNK - 1 - in_chunk).astype(jnp.uint32),
                      1.0, 0.0)
    for blk in range(la.shape[0] // G_BLOCK):
        sub = slice(blk * G_BLOCK, (blk + 1) * G_BLOCK)
        dst = slice(rows.start + blk * G_BLOCK, rows.start + (blk + 1) * G_BLOCK)
        for tri, lo_col, out_ref in ((tri_f, 0, cumf_ref), (tri_b, G_QK, cumb_ref)):
            hi, lo = _split_bf16(la[sub, lo_col:lo_col + G_QK])
            cum = _dot(tri.astype(BF16), jnp.concatenate([hi, lo], axis=1))
            out_ref[dst, :] = cum[:, :G_QK] + cum[:, G_QK:]
            yield


def _out_proj_kernel(*refs, latent, chains):
    step = refs[3].shape[0] // chains
    _interleave(*[_out_proj_chain(slice(i * step, (i + 1) * step), refs, latent) for i in range(chains)])


def _out_proj(oa, ob, w0, x, gate, norm_g, sc, sh, w, wab, bab, latent, tm):
    bn, t, _ = x.shape
    fm = pl.BlockSpec((None, A_WIDTH, tm), lambda j, b: (b, 0, j))

    def tk(width, dtype):
        return (pl.BlockSpec((None, tm, width), lambda j, b: (b, j, 0)),
                jax.ShapeDtypeStruct((bn, t, width), dtype))

    const = lambda rows, cols: pl.BlockSpec((rows, cols), lambda j, b: (0, 0))
    x_spec = tk(D_MODEL, F32)
    tail = [tk(G_QK, BF16), tk(G_V, BF16), tk(G_QK, F32), tk(G_QK, F32)]
    if latent:
        outs = [x_spec, tk(G_QK, BF16), tail[0], tail[1], tk(G_V, BF16), tail[2], tail[3]]
    else:
        outs = tail
    return pl.pallas_call(
        functools.partial(_out_proj_kernel, latent=latent, chains=2 if latent else 1),
        grid=(t // tm, bn),
        in_specs=[
            fm, fm, const(D_MODEL, D_MODEL), x_spec[0], _mod_spec(latent),
            const(1, D_MODEL), _mod_spec(latent), _mod_spec(latent),
            const(D_MODEL, ODD_PAD), const(128, 2 * G_QK), const(1, 2 * G_QK),
        ],
        out_specs=[o[0] for o in outs],
        out_shape=[o[1] for o in outs],
        compiler_params=_params("arbitrary", "arbitrary"),
        name="out_proj" if latent else "ctx_out_proj",
    )(oa, ob, w0, x, gate, norm_g, sc, sh, w, wab, bab)


def _gla_block(q, k, v, b, state, reverse, with_out):
    n = G_BLOCK // G_CHUNK
    ends = [(j * G_CHUNK if reverse else (j + 1) * G_CHUNK - 1) for j in range(n)]
    b_last = jnp.concatenate(
        [jnp.broadcast_to(b[e:e + 1, :], (G_CHUNK, G_DK)) for e in ends], axis=0)
    kf = k.astype(F32)
    k_state = (kf * jnp.exp(b_last - b)).astype(BF16)
    if with_out:
        qd = (q.astype(F32) * jnp.exp(b)).astype(BF16)
        kd = (kf * jnp.exp(-b)).astype(BF16)
    yield
    if with_out:
        att = _dot_nt(qd, kd)
        yield
    chunks = [slice(j * G_CHUNK, (j + 1) * G_CHUNK) for j in range(n)]
    d_state = [_dot_tn(k_state[sl], v[sl]) for sl in chunks]
    decay_t = jnp.exp(b_last).T
    yield
    if with_out:
        r = lax.broadcasted_iota(jnp.int32, (G_BLOCK, G_BLOCK), 0)
        c = lax.broadcasted_iota(jnp.int32, (G_BLOCK, G_BLOCK), 1)
        in_chunk = r % G_CHUNK
        if reverse:
            causal = (c - r).astype(jnp.uint32) <= (G_CHUNK - 1 - in_chunk).astype(jnp.uint32)
        else:
            causal = (r - c).astype(jnp.uint32) <= in_chunk.astype(jnp.uint32)
        intra = _dot(jnp.where(causal, att, 0.0).astype(BF16), v)
        yield
    outs = [None] * n
    for j in (reversed(range(n)) if reverse else range(n)):
        if with_out:
            outs[j] = intra[chunks[j]] + _dot(qd[chunks[j]], state.astype(BF16))
        decay = jnp.broadcast_to(decay_t[:, ends[j]:ends[j] + 1], (G_DK, G_DV))
        state = state * decay + d_state[j]
        yield
    return (jnp.concatenate(outs, axis=0) if with_out else None), state


def _gla_kernel(q_ref, k_ref, v_ref, cumf_ref, cumb_ref, g_ref,
                kc_ref, vc_ref, cumfc_ref, cumbc_ref, w_ref, o_ref, of_ref, ob_ref, *, heads):
    n_lat = q_ref.shape[0] // G_BLOCK
    n_ctx = kc_ref.shape[0] // G_BLOCK
    qk = [slice(G_DK * h, G_DK * (h + 1)) for h in range(heads)]
    vv = [slice(G_DV * h, G_DV * (h + 1)) for h in range(heads)]

    def rows(i):
        return pl.ds(pl.multiple_of(i * G_BLOCK, G_BLOCK), G_BLOCK)

    states = [jnp.zeros((G_DK, G_DV), F32)] * (2 * heads)
    for i in range(n_ctx):
        sl_f = slice(i * G_BLOCK, (i + 1) * G_BLOCK)
        sl_b = slice((n_ctx - 1 - i) * G_BLOCK, (n_ctx - i) * G_BLOCK)
        gens = []
        for h in range(heads):
            gens.append(_gla_block(None, kc_ref[sl_f, qk[h]], vc_ref[sl_f, vv[h]],
                                   cumfc_ref[sl_f, qk[h]], states[2 * h], False, False))
            gens.append(_gla_block(None, kc_ref[sl_b, qk[h]], vc_ref[sl_b, vv[h]],
                                   cumbc_ref[sl_b, qk[h]], states[2 * h + 1], True, False))
        states = [st for _, st in _interleave(*gens)]

    def finish(i, h, o, other_ref):
        o = o + other_ref[rows(i), vv[h]]
        rn = lax.rsqrt(jnp.mean(o * o, axis=-1, keepdims=True) + EPS)
        o = (o * rn) * w_ref[...]
        o_ref[rows(i), vv[h]] = (o * g_ref[rows(i), vv[h]].astype(F32)).astype(BF16)

    def body(i, states, second_half):
        i_b = n_lat - 1 - i
        gens = []
        for h in range(heads):
            gens.append(_gla_block(q_ref[rows(i), qk[h]], k_ref[rows(i), qk[h]], v_ref[rows(i), vv[h]],
                                   cumf_ref[rows(i), qk[h]], states[2 * h], False, True))
            gens.append(_gla_block(q_ref[rows(i_b), qk[h]], k_ref[rows(i_b), qk[h]],
                                   v_ref[rows(i_b), vv[h]], cumb_ref[rows(i_b), qk[h]],
                                   states[2 * h + 1], True, True))
        results = _interleave(*gens)
        for h in range(heads):
            (o_f, _), (o_b, _) = results[2 * h], results[2 * h + 1]
            if second_half:
                finish(i, h, o_f, ob_ref)
                finish(i_b, h, o_b, of_ref)
            else:
                of_ref[rows(i), vv[h]] = o_f
                ob_ref[rows(i_b), vv[h]] = o_b
        return tuple(st for _, st in results)

    assert n_lat % 2 == 0
    states = lax.fori_loop(0, n_lat // 2, functools.partial(body, second_half=False),
                           tuple(states), unroll=True)
    lax.fori_loop(n_lat // 2, n_lat, functools.partial(body, second_half=True), states, unroll=True)


def _gla(q, k, v, cumf, cumb, gate, kc, vc, cumfc, cumbc, w_norm):
    bn, t, _ = q.shape
    ctx_len = kc.shape[1]
    heads = GLA_HEADS_PER_STEP
    lat = lambda width: pl.BlockSpec((None, t, heads * width), lambda b, h: (b, 0, h))
    ctx = lambda width: pl.BlockSpec((None, ctx_len, heads * width), lambda b, h: (b, 0, h))
    return pl.pallas_call(
        functools.partial(_gla_kernel, heads=heads),
        grid=(bn, G_HEADS // heads),
        in_specs=[lat(G_DK), lat(G_DK), lat(G_DV), lat(G_DK), lat(G_DK), lat(G_DV),
                  ctx(G_DK), ctx(G_DV), ctx(G_DK), ctx(G_DK),
                  pl.BlockSpec((1, G_DV), lambda b, h: (0, 0))],
        out_specs=lat(G_DV),
        out_shape=jax.ShapeDtypeStruct((bn, t, G_V), BF16),
        scratch_shapes=[pltpu.VMEM((t, heads * G_DV), F32), pltpu.VMEM((t, heads * G_DV), F32)],
        compiler_params=_params("arbitrary", "arbitrary"),
        name="gla",
    )(q, k, v, cumf, cumb, gate, kc, vc, cumfc, cumbc, w_norm)


def _odd_out_kernel(o_ref, w_ref, x_ref, gt_ref, out_ref):
    out_ref[...] = x_ref[...] + gt_ref[...] * _dot(o_ref[...], w_ref[...])


def _odd_out(o, w, x, gate, tm):
    bn, t, _ = x.shape
    xs = pl.BlockSpec((None, tm, D_MODEL), lambda j, b: (b, j, 0))
    return pl.pallas_call(
        _odd_out_kernel,
        grid=(t // tm, bn),
        in_specs=[xs, pl.BlockSpec((D_MODEL, D_MODEL), lambda j, b: (0, 0)), xs, _mod_spec(True)],
        out_specs=xs,
        out_shape=jax.ShapeDtypeStruct(x.shape, F32),
        compiler_params=_params("arbitrary", "arbitrary"),
        name="odd_out",
    )(o, w, x, gate)


TOKEN_TILE = 1024
OUT_PROJ_TILE = 512
FINAL_TILE = 2048


def _token_tile(t):
    return TOKEN_TILE if t % TOKEN_TILE == 0 else t


def kernel(x, c, ctx, c_ctx, adaln_w, adaln_b, norm_g, w_out, ab_w_in, a_q_norm, a_k_norm, a_sink,
           b_q_norm, b_k_norm, b_lambda_q1, b_lambda_k1, b_lambda_q2, b_lambda_k2, b_subln,
           gla_w_in, gla_wa_f, gla_ba_f, gla_wa_b, gla_ba_b, gla_out_norm):
    bn, t, _ = x.shape
    ctx_len = ctx.shape[1]
    depth = adaln_w.shape[0]
    assert depth == 2

    pad_rows = (-(bn + 1)) % 8
    cond = jnp.concatenate([c, c_ctx[None, :], jnp.zeros((pad_rows, D_MODEL), F32)], axis=0)
    mod = _adaln(cond, adaln_w, adaln_b[:, None, :])

    def mods(layer):
        m = mod[layer]
        shift, scale, gate = (m[:, i * D_MODEL:(i + 1) * D_MODEL] for i in range(3))
        per_x = tuple(v[:bn, None, :] for v in (shift, scale, gate))
        per_c = tuple(v[bn:bn + 1, None, :] for v in (shift, scale, gate))
        return per_x, per_c

    (shx, scx, gtx), (shc, scc, gtc) = mods(0)
    lambda_init = 0.8 - 0.6 * math.exp(-0.3 * 0)
    wt = ab_w_in[0].T.astype(BF16)
    gains = [a_q_norm[0], a_k_norm[0], b_q_norm[0], b_k_norm[0]]
    q_scale = [ATTN_SCALE * LOG2E, 1.0, ATTN_SCALE * LOG2E, 1.0]
    tab_x = _rope_tables(gains, t, True, q_scale)
    tab_c = _rope_tables(gains, ctx_len, False, q_scale)
    g0 = norm_g[0][None, :]
    qa, ka, va, ga, qb, kb, vb, gb = _even_proj(x, g0, scx, shx, wt, tab_x, True, _token_tile(t))
    qac, kac, vac, gac, qbc, kbc, vbc, gbc = _even_proj(ctx, g0, scc, shc, wt, tab_c, False,
                                                        _token_tile(ctx_len))

    sink = a_sink[0].astype(F32) * LOG2E
    oa = _win_attn(qa, ka, va, kac, vac, ga, jnp.repeat(sink, Q_BLOCK)[None, :], True)
    oac = _win_attn(qac, None, None, kac, vac, gac, jnp.repeat(sink, ctx_len)[None, :], False)

    lam = (jnp.exp(jnp.sum(b_lambda_q1[0].astype(F32) * b_lambda_k1[0].astype(F32)))
           - jnp.exp(jnp.sum(b_lambda_q2[0].astype(F32) * b_lambda_k2[0].astype(F32))) + lambda_init)
    tq = 256
    w_sub = jnp.broadcast_to((b_subln[0].astype(F32) * (1.0 - lambda_init))[:, None], (B_VDIM, tq))
    lam_row = jnp.broadcast_to(lam, (1, tq)).astype(F32)
    ob = _diff_attn(qb, kb, vb, kbc, vbc, gb, w_sub, lam_row, tq)
    obc = _ctx_diff_attn(qbc, kbc, vbc, gbc, w_sub, lam_row)

    w0 = w_out[0].astype(BF16)
    (shx1, scx1, gtx1), (shc1, scc1, _) = mods(1)
    w1 = jnp.pad(gla_w_in[0], ((0, 0), (0, ODD_PAD - gla_w_in.shape[2]))).astype(BF16)
    wab = jnp.zeros((128, 2 * G_QK), F32)
    wab = wab.at[0:G_RANK, 0:G_QK].set(gla_wa_f[0])
    wab = wab.at[G_RANK:2 * G_RANK, G_QK:].set(gla_wa_b[0]).astype(BF16)
    bab = jnp.concatenate([gla_ba_f[0], gla_ba_b[0]])[None, :].astype(F32)
    g1 = norm_g[1][None, :]
    x, q, k, v, gate, cumf, cumb = _out_proj(oa, ob, w0, x, gtx, g1, scx1, shx1, w1, wab, bab,
                                             True, OUT_PROJ_TILE if t % OUT_PROJ_TILE == 0 else t)
    kc, vc, cumfc, cumbc = _out_proj(oac, obc, w0, ctx, gtc, g1, scc1, shc1, w1, wab, bab,
                                     False, _token_tile(ctx_len))
    o = _gla(q, k, v, cumf, cumb, gate, kc, vc, cumfc, cumbc, gla_out_norm[0][None, :].astype(F32))
    return _odd_out(o, w_out[1].astype(BF16), x, gtx1, FINAL_TILE if t % FINAL_TILE == 0 else t)
```
